```python
import math
import jax, jax.numpy as jnp
from jax import lax
import numpy as np

D_MODEL = 1024
BATCH = 2
SEQ = 8192
DEPTH = 2

PLE_DIM = 256
N_A = DEPTH // 2
N_B = DEPTH - N_A
SB_HEAD_DIM = 64
SB_HEADS = D_MODEL // SB_HEAD_DIM
SB_WIDTH = SB_HEADS * SB_HEAD_DIM
DIFF_QK_DIM = 64
DIFF_V_DIM = 2 * DIFF_QK_DIM
DIFF_HEADS = D_MODEL // DIFF_V_DIM
DIFF_QK_WIDTH = DIFF_HEADS * DIFF_QK_DIM
DIFF_WIDTH = DIFF_HEADS * DIFF_V_DIM
N_BUCKETS = 32
MAX_DISTANCE = 128
BLOCK_Q = 128
EPS = 1e-6
NEG_INF = -1e30

kernel_name = "yoco_stickbreak_diffattn_hybrid"


def rmsnorm(x, g):
    xf = x.astype(jnp.float32)
    y = xf * lax.rsqrt(jnp.mean(xf * xf, axis=-1, keepdims=True) + EPS)
    return (y * g.astype(jnp.float32)).astype(x.dtype)


def t5_bucket(dist):
    n = jnp.maximum(dist, 0)
    max_exact = N_BUCKETS // 2
    nf = jnp.maximum(n, 1).astype(jnp.float32)
    large = max_exact + (jnp.log(nf / max_exact) / math.log(MAX_DISTANCE / max_exact)
                         * (N_BUCKETS - max_exact)).astype(jnp.int32)
    large = jnp.minimum(large, N_BUCKETS - 1)
    return jnp.where(n < max_exact, n, large)


def to_blocks(a):
    b, s, h, d = a.shape
    return a.reshape(b, s // BLOCK_Q, BLOCK_Q, h, d).transpose(1, 0, 2, 3, 4)


def from_blocks(a):
    nb, b, tq, h, d = a.shape
    return a.transpose(1, 0, 2, 3, 4).reshape(b, nb * tq, h, d)


def stick_breaking_attention(q, k, v):
    s_len = k.shape[1]
    scale = 1.0 / math.sqrt(q.shape[-1])
    key_pos = jnp.arange(s_len)

    def block(args):
        qb, t0 = args
        z = jnp.einsum('bqhd,bkhd->bhqk', qb, k) * scale
        q_pos = t0 + jnp.arange(BLOCK_Q)
        past = key_pos[None, :] < q_pos[:, None]
        log_1m = jnp.where(past, jax.nn.log_sigmoid(-z), 0.0)
        rev = lax.cumsum(log_1m, axis=3, reverse=True)
        after = jnp.concatenate([rev[..., 1:], jnp.zeros_like(rev[..., :1])], axis=-1)
        w = jnp.where(past, jnp.exp(jax.nn.log_sigmoid(z) + after), 0.0)
        return jnp.einsum('bhqk,bkhd->bqhd', w, v)

    nb = s_len // BLOCK_Q
    t0s = jnp.arange(nb, dtype=jnp.int32) * BLOCK_Q
    return from_blocks(lax.map(block, (to_blocks(q), t0s)))


def differential_attention(q1, q2, k1, k2, v, lam, rel_bias):
    s_len = k1.shape[1]
    scale = 1.0 / math.sqrt(q1.shape[-1])
    key_pos = jnp.arange(s_len)
    table = rel_bias.astype(jnp.float32)

    def block(args):
        q1b, q2b, t0 = args
        q_pos = t0 + jnp.arange(BLOCK_Q)
        dist = q_pos[:, None] - key_pos[None, :]
        causal = dist >= 0
        bias = jnp.transpose(table[t5_bucket(dist)], (2, 0, 1))
        s1 = jnp.einsum('bqhd,bkhd->bhqk', q1b, k1) * scale + bias
        s2 = jnp.einsum('bqhd,bkhd->bhqk', q2b, k2) * scale + bias
        a1 = jax.nn.softmax(jnp.where(causal, s1, NEG_INF), axis=-1)
        a2 = jax.nn.softmax(jnp.where(causal, s2, NEG_INF), axis=-1)
        return jnp.einsum('bhqk,bkhe->bqhe', a1 - lam * a2, v)

    nb = s_len // BLOCK_Q
    t0s = jnp.arange(nb, dtype=jnp.int32) * BLOCK_Q
    return from_blocks(lax.map(block, (to_blocks(q1), to_blocks(q2), t0s)))


def setup_inputs(seed: int = 0) -> dict:
    key = jax.random.key(seed)
    ks = jax.random.split(key, 18)

    def nrm(k, shape, scale):
        return jax.random.normal(k, shape, jnp.float32) * scale

    def gain(k, shape):
        return 1.0 + 0.02 * jax.random.normal(k, shape, jnp.float32)

    fan = D_MODEL ** -0.5
    return {
        "x": nrm(ks[0], (BATCH, SEQ, D_MODEL), 1.0),
        "p": nrm(ks[1], (DEPTH, BATCH, SEQ, PLE_DIM), 1.0),
        "attn_norm": gain(ks[2], (DEPTH, D_MODEL)),
        "w_in_a": nrm(ks[3], (N_A, D_MODEL, 4 * SB_WIDTH), fan),
        "w_out_a": nrm(ks[4], (N_A, SB_WIDTH, D_MODEL), SB_WIDTH ** -0.5),
        "kv_norm": gain(ks[5], (D_MODEL,)),
        "w_kv": nrm(ks[6], (D_MODEL, 2 * DIFF_QK_WIDTH + DIFF_WIDTH), fan),
        "w_q_b": nrm(ks[7], (N_B, D_MODEL, 2 * DIFF_QK_WIDTH + DIFF_WIDTH), fan),
        "diff_lambda": nrm(ks[8], (N_B, 4, DIFF_QK_DIM), 0.1),
        "diff_subln": gain(ks[9], (N_B, DIFF_V_DIM)),
        "w_out_b": nrm(ks[10], (N_B, DIFF_WIDTH, D_MODEL), DIFF_WIDTH ** -0.5),
        "rel_bias": nrm(ks[11], (N_BUCKETS, DIFF_HEADS), 0.5),
        "ple_norm": gain(ks[12], (DEPTH, D_MODEL)),
        "w_ple_proj": nrm(ks[13], (DEPTH, PLE_DIM, D_MODEL), PLE_DIM ** -0.5),
        "w_ple_gate": nrm(ks[14], (DEPTH, D_MODEL, D_MODEL), fan),
        "final_norm": gain(ks[15], (D_MODEL,)),
    }


def reference(x, p, attn_norm, w_in_a, w_out_a, kv_norm, w_kv, w_q_b, diff_lambda, diff_subln,
              w_out_b, rel_bias, ple_norm, w_ple_proj, w_ple_gate, final_norm):
    b, s, _ = x.shape
    h = x
    shared_k1 = shared_k2 = shared_v = None
    for i in range(DEPTH):
        if i < N_A:
            hn = rmsnorm(h, attn_norm[i])
            proj = hn @ w_in_a[i]
            q, k, v, g = jnp.split(proj, 4, axis=-1)
            q, k, v = (t.reshape(b, s, SB_HEADS, SB_HEAD_DIM).astype(jnp.float32) for t in (q, k, v))
            o = stick_breaking_attention(q, k, v).reshape(b, s, SB_WIDTH).astype(h.dtype)
            h = h + (o * jax.nn.silu(g)) @ w_out_a[i]
        else:
            j = i - N_A
            if i == N_A:
                kv = rmsnorm(h, kv_norm) @ w_kv
                k1, k2, vv = jnp.split(kv, [DIFF_QK_WIDTH, 2 * DIFF_QK_WIDTH], axis=-1)
                shared_k1 = k1.reshape(b, s, DIFF_HEADS, DIFF_QK_DIM).astype(jnp.float32)
                shared_k2 = k2.reshape(b, s, DIFF_HEADS, DIFF_QK_DIM).astype(jnp.float32)
                shared_v = vv.reshape(b, s, DIFF_HEADS, DIFF_V_DIM).astype(jnp.float32)
            hn = rmsnorm(h, attn_norm[i])
            qg = hn @ w_q_b[j]
            q1, q2, g = jnp.split(qg, [DIFF_QK_WIDTH, 2 * DIFF_QK_WIDTH], axis=-1)
            q1 = q1.reshape(b, s, DIFF_HEADS, DIFF_QK_DIM).astype(jnp.float32)
            q2 = q2.reshape(b, s, DIFF_HEADS, DIFF_QK_DIM).astype(jnp.float32)
            lam_init = 0.8 - 0.6 * math.exp(-0.3 * i)
            lp = diff_lambda[j].astype(jnp.float32)
            lam = jnp.exp(jnp.sum(lp[0] * lp[1])) - jnp.exp(jnp.sum(lp[2] * lp[3])) + lam_init
            o = differential_attention(q1, q2, shared_k1, shared_k2, shared_v, lam, rel_bias)
            o = rmsnorm(o, diff_subln[j]) * (1.0 - lam_init)
            o = o.reshape(b, s, DIFF_WIDTH).astype(h.dtype)
            h = h + (o * jax.nn.silu(g)) @ w_out_b[j]
        gate = jax.nn.sigmoid(rmsnorm(h, ple_norm[i]) @ w_ple_gate[i])
        h = h + gate * (p[i] @ w_ple_proj[i])
    return rmsnorm(h, final_norm)
```

```python
import functools
import math

import jax
import jax.numpy as jnp
from jax import lax
from jax.experimental import pallas as pl
from jax.experimental.pallas import tpu as pltpu

D_MODEL = 1024
PLE_DIM = 256
SB_HEAD_DIM = 64
DIFF_QK_DIM = 64
DIFF_V_DIM = 128
DIFF_HEADS = 8
N_BUCKETS = 32
MAX_DISTANCE = 128
EPS = 1e-6
NEG_INF = -1e30

LANES = 128
T = 256
TM_PROJ = 512
TM_MID = 256
VMEM_LIMIT = 56 * 1024 * 1024

F32 = jnp.float32
BF16 = jnp.bfloat16
NT_DIMS = (((1,), (1,)), ((), ()))


def _rms(x, gain):
    ms = jnp.mean(x * x, axis=-1, keepdims=True)
    return x * lax.rsqrt(ms + EPS) * gain


def _silu(g):
    return g * jax.nn.sigmoid(g)


def _dot(a, b):
    return jnp.dot(a, b, preferred_element_type=F32)


def _dot_nt(a, b):
    return lax.dot_general(a, b, NT_DIMS, preferred_element_type=F32)


def _store_token_major_t(ref, val_t, scale=None):
    for c in range(val_t.shape[1] // T):
        blk = val_t[:, c * T:(c + 1) * T]
        if scale is not None:
            blk = blk * scale
        ref[c] = blk.astype(BF16)


def _bias_lam_kernel(table_ref, lp_ref, bias_ref, lam_ref, *, lam_init):
    h = pl.program_id(0)
    key = lax.broadcasted_iota(jnp.int32, (T, T), 0)
    qry = lax.broadcasted_iota(jnp.int32, (T, T), 1)
    max_exact = N_BUCKETS // 2
    far = table_ref[N_BUCKETS - 1, h]
    for d in range(2):
        dist = qry - key + d * T
        n = jnp.maximum(dist, 0)
        nf = jnp.maximum(n, 1).astype(F32)
        large = max_exact + (jnp.log(nf / max_exact) / math.log(MAX_DISTANCE / max_exact)
                             * (N_BUCKETS - max_exact)).astype(jnp.int32)
        large = jnp.minimum(large, N_BUCKETS - 1)
        bucket = jnp.where(n < max_exact, n, large)
        val = jnp.zeros((T, T), F32)
        for b in range(N_BUCKETS):
            val = jnp.where(bucket == b, table_ref[b, h], val)
        bias_ref[0, d] = jnp.where(dist >= 0, val - far, NEG_INF)

    lp = lp_ref[...]
    s1 = jnp.sum(lp[0:1, :] * lp[1:2, :], axis=-1, keepdims=True)
    s2 = jnp.sum(lp[2:3, :] * lp[3:4, :], axis=-1, keepdims=True)
    lam = jnp.exp(s1) - jnp.exp(s2) + lam_init
    lam_ref[...] = jnp.broadcast_to(lam, lam_ref.shape)


def _bias_lam(rel_bias, lp, lam_init):
    return pl.pallas_call(
        functools.partial(_bias_lam_kernel, lam_init=lam_init),
        grid=(DIFF_HEADS,),
        in_specs=[pl.BlockSpec(memory_space=pltpu.SMEM),
                  pl.BlockSpec((4, DIFF_QK_DIM), lambda h: (0, 0))],
        out_specs=[pl.BlockSpec((1, 2, T, T), lambda h: (h, 0, 0, 0)),
                   pl.BlockSpec((8, LANES), lambda h: (0, 0))],
        out_shape=[jax.ShapeDtypeStruct((DIFF_HEADS, 2, T, T), F32),
                   jax.ShapeDtypeStruct((8, LANES), F32)],
        name="bias_lam",
    )(rel_bias, lp)


def _inproj_a_kernel(x_ref, gain_ref, wkg_ref, wqvt_ref, k_ref, g_ref, qt_ref, vt_ref, *, scale):
    hn = _rms(x_ref[...], gain_ref[...]).astype(BF16)
    k_ref[...] = _dot(hn, wkg_ref[:, :D_MODEL]).astype(BF16)
    g_ref[...] = _dot(hn, wkg_ref[:, D_MODEL:])
    _store_token_major_t(qt_ref, _dot_nt(wqvt_ref[:D_MODEL, :], hn), scale)
    _store_token_major_t(vt_ref, _dot_nt(wqvt_ref[D_MODEL:, :], hn))


def _const_spec(shape):
    return pl.BlockSpec(shape, lambda *_: (0,) * len(shape))


def _inproj_a(x2, gain, wkg, wqvt, scale):
    n_tok = x2.shape[0]
    tm = TM_PROJ
    return pl.pallas_call(
        functools.partial(_inproj_a_kernel, scale=scale),
        grid=(n_tok // tm,),
        in_specs=[pl.BlockSpec((tm, D_MODEL), lambda i: (i, 0)),
                  _const_spec((1, D_MODEL)),
                  _const_spec((D_MODEL, 2 * D_MODEL)),
                  _const_spec((2 * D_MODEL, D_MODEL))],
        out_specs=[pl.BlockSpec((tm, D_MODEL), lambda i: (i, 0)),
                   pl.BlockSpec((tm, D_MODEL), lambda i: (i, 0)),
                   pl.BlockSpec((tm // T, D_MODEL, T), lambda i: (i, 0, 0)),
                   pl.BlockSpec((tm // T, D_MODEL, T), lambda i: (i, 0, 0))],
        out_shape=[jax.ShapeDtypeStruct((n_tok, D_MODEL), BF16),
                   jax.ShapeDtypeStruct((n_tok, D_MODEL), F32),
                   jax.ShapeDtypeStruct((n_tok // T, D_MODEL, T), BF16),
                   jax.ShapeDtypeStruct((n_tok // T, D_MODEL, T), BF16)],
        compiler_params=pltpu.CompilerParams(dimension_semantics=("arbitrary",),
                                             vmem_limit_bytes=VMEM_LIMIT),
        name="inproj_a",
    )(x2, gain, wkg, wqvt)


def _sb_attn_kernel(qt_ref, k_ref, vt_ref, o_ref):
    i = pl.program_id(2)
    hd = SB_HEAD_DIM
    qt = qt_ref[0]
    row = lax.broadcasted_iota(jnp.int32, qt.shape, 0)
    zero = jnp.zeros_like(qt)
    q_heads = (jnp.where(row < hd, qt, zero), jnp.where(row >= hd, qt, zero))
    key = lax.broadcasted_iota(jnp.int32, (T, T), 0)
    qry = lax.broadcasted_iota(jnp.int32, (T, T), 1)
    upper = jnp.where(qry > key, 1.0, 0.0).astype(BF16)
    past = key < qry

    def tile(j, carry, diag):
        start = pl.multiple_of(j * T, T)
        kb = k_ref[pl.ds(start, T), :]
        out = []
        for hh in range(2):
            acc, run = carry[hh]
            z = _dot(kb, q_heads[hh])
            soft = jnp.log(1.0 + jnp.exp(-jnp.abs(z)))
            sp = jnp.maximum(z, 0.0) + soft
            logsig = jnp.minimum(z, 0.0) - soft
            if diag:
                sp = jnp.where(past, sp, 0.0)
            c = _dot(upper, sp.astype(BF16))
            w = jnp.exp(logsig - c)
            if diag:
                w = jnp.where(past, w, 0.0)
            vb = vt_ref[j, hh * hd:(hh + 1) * hd, :]
            pv = _dot(vb, w.astype(BF16))
            acc = acc + pv * jnp.exp(-run)
            run = run + c[0:1, :] + sp[0:1, :]
            out.append((acc, run))
        return tuple(out)

    init = tuple((jnp.zeros((hd, T), F32), jnp.zeros((1, T), F32)) for _ in range(2))
    carry = tile(i, init, True)
    carry = lax.fori_loop(0, i, lambda t, c: tile(i - 1 - t, c, False), carry)
    o_t = jnp.concatenate([carry[0][0], carry[1][0]], axis=0)
    o_ref[...] = o_t.T


def _sb_attn(qt, k, vt, batch, seq):
    nq = seq // T
    n_tok = batch * seq
    pairs = D_MODEL // LANES
    return pl.pallas_call(
        _sb_attn_kernel,
        grid=(batch, pairs, nq),
        in_specs=[pl.BlockSpec((1, LANES, T), lambda b, h, i: (b * nq + i, h, 0)),
                  pl.BlockSpec((seq, LANES), lambda b, h, i: (b, h)),
                  pl.BlockSpec((nq, LANES, T), lambda b, h, i: (b, h, 0))],
        out_specs=pl.BlockSpec((T, LANES), lambda b, h, i: (b * nq + i, h)),
        out_shape=jax.ShapeDtypeStruct((n_tok, D_MODEL), F32),
        compiler_params=pltpu.CompilerParams(dimension_semantics=("arbitrary",) * 3,
                                             vmem_limit_bytes=VMEM_LIMIT),
        name="sb_attn",
    )(qt, k, vt)


def _ple_update(h, p, ple_gain, w_gate, w_ple):
    gate = jax.nn.sigmoid(_dot(_rms(h, ple_gain).astype(BF16), w_gate))
    return h + gate * _dot(p.astype(BF16), w_ple)


def _mid_kernel(x_ref, o_ref, g_ref, p_ref, ple_gain_ref, kv_gain_ref, q_gain_ref,
                w_out_ref, w_gate_ref, w_ple_ref, w_k_ref, w_vt_ref, w_qt_ref, w_g_ref,
                h_ref, k_ref, vt_ref, qt_ref, g2_ref, *, scale):
    y = (o_ref[...] * _silu(g_ref[...])).astype(BF16)
    h = x_ref[...] + _dot(y, w_out_ref[...])
    h = _ple_update(h, p_ref[...], ple_gain_ref[...], w_gate_ref[...], w_ple_ref[...])
    h_ref[...] = h
    hk = _rms(h, kv_gain_ref[...]).astype(BF16)
    k_ref[...] = _dot(hk, w_k_ref[...]).astype(BF16)
    _store_token_major_t(vt_ref, _dot_nt(w_vt_ref[...], hk))
    hq = _rms(h, q_gain_ref[...]).astype(BF16)
    _store_token_major_t(qt_ref, _dot_nt(w_qt_ref[...], hq), scale)
    g2_ref[...] = _dot(hq, w_g_ref[...])


def _mid(x2, o, g, p, ple_gain, kv_gain, q_gain, w_out, w_gate, w_ple, w_k, w_vt, w_qt, w_g, scale):
    n_tok = x2.shape[0]
    tm = TM_MID
    row = lambda w: pl.BlockSpec((tm, w), lambda i: (i, 0))
    sq = _const_spec((D_MODEL, D_MODEL))
    vec = _const_spec((1, D_MODEL))
    tmaj = pl.BlockSpec((tm // T, D_MODEL, T), lambda i: (i, 0, 0))
    return pl.pallas_call(
        functools.partial(_mid_kernel, scale=scale),
        grid=(n_tok // tm,),
        in_specs=[row(D_MODEL), row(D_MODEL), row(D_MODEL), row(PLE_DIM), vec, vec, vec,
                  sq, sq, _const_spec((PLE_DIM, D_MODEL)), sq, sq, sq, sq],
        out_specs=[row(D_MODEL), row(D_MODEL), tmaj, tmaj, row(D_MODEL)],
        out_shape=[jax.ShapeDtypeStruct((n_tok, D_MODEL), F32),
                   jax.ShapeDtypeStruct((n_tok, D_MODEL), BF16),
                   jax.ShapeDtypeStruct((n_tok // T, D_MODEL, T), BF16),
                   jax.ShapeDtypeStruct((n_tok // T, D_MODEL, T), BF16),
                   jax.ShapeDtypeStruct((n_tok, D_MODEL), F32)],
        compiler_params=pltpu.CompilerParams(dimension_semantics=("arbitrary",),
                                             vmem_limit_bytes=VMEM_LIMIT),
        name="mid",
    )(x2, o, g, p, ple_gain, kv_gain, q_gain, w_out, w_gate, w_ple, w_k, w_vt, w_qt, w_g)


def _diff_attn_kernel(lam_ref, qt_ref, k_ref, vt_ref, bias_ref, subln_ref, o_ref, *, out_scale):
    i = pl.program_id(2)
    dk = DIFF_QK_DIM
    qt = qt_ref[0]
    row = lax.broadcasted_iota(jnp.int32, qt.shape, 0)
    zero = jnp.zeros_like(qt)
    q_maps = (jnp.where(row < dk, qt, zero), jnp.where(row >= dk, qt, zero))

    def tile(j, carry, bias):
        start = pl.multiple_of(j * T, T)
        kb = k_ref[pl.ds(start, T), :]
        vb = vt_ref[j]
        out = []
        for mm in range(2):
            m, l, acc = carry[mm]
            s = _dot(kb, q_maps[mm])
            if bias is not None:
                s = s + bias
            m_new = jnp.maximum(m, jnp.max(s, axis=0, keepdims=True))
            alpha = jnp.exp(m - m_new)
            p = jnp.exp(s - m_new)
            l = l * alpha + jnp.sum(p, axis=0, keepdims=True)
            acc = acc * alpha + _dot(vb, p.astype(BF16))
            out.append((m_new, l, acc))
        return tuple(out)

    init = tuple((jnp.full((1, T), NEG_INF, F32), jnp.zeros((1, T), F32), jnp.zeros((DIFF_V_DIM, T), F32))
                 for _ in range(2))
    carry = tile(i, init, bias_ref[0, 0])
    near = jnp.minimum(i, 1)
    carry = lax.fori_loop(0, near, lambda t, c: tile(i - 1, c, bias_ref[0, 1]), carry)
    carry = lax.fori_loop(0, i - near, lambda t, c: tile(i - 2 - t, c, None), carry)

    (_, l1, acc1), (_, l2, acc2) = carry
    lam = lam_ref[0, 0]
    o = acc1 * (1.0 / l1) - lam * (acc2 * (1.0 / l2))
    ms = jnp.mean(o * o, axis=0, keepdims=True)
    y = o * lax.rsqrt(ms + EPS) * subln_ref[...] * out_scale
    o_ref[...] = y.T


def _diff_attn(lam, qt, k, vt, bias, subln, batch, seq, out_scale):
    nq = seq // T
    n_tok = batch * seq
    return pl.pallas_call(
        functools.partial(_diff_attn_kernel, out_scale=out_scale),
        grid=(batch, DIFF_HEADS, nq),
        in_specs=[pl.BlockSpec(memory_space=pltpu.SMEM),
                  pl.BlockSpec((1, LANES, T), lambda b, h, i: (b * nq + i, h, 0)),
                  pl.BlockSpec((seq, LANES), lambda b, h, i: (b, h)),
                  pl.BlockSpec((nq, LANES, T), lambda b, h, i: (b, h, 0)),
                  pl.BlockSpec((1, 2, T, T), lambda b, h, i: (h, 0, 0, 0)),
                  pl.BlockSpec((DIFF_V_DIM, 1), lambda b, h, i: (0, 0))],
        out_specs=pl.BlockSpec((T, LANES), lambda b, h, i: (b * nq + i, h)),
        out_shape=jax.ShapeDtypeStruct((n_tok, D_MODEL), F32),
        compiler_params=pltpu.CompilerParams(dimension_semantics=("arbitrary",) * 3,
                                             vmem_limit_bytes=VMEM_LIMIT),
        name="diff_attn",
    )(lam, qt, k, vt, bias, subln)


def _final_kernel(h_ref, o_ref, g_ref, p_ref, ple_gain_ref, fin_gain_ref,
                  w_out_ref, w_gate_ref, w_ple_ref, out_ref):
    y = (o_ref[...] * _silu(g_ref[...])).astype(BF16)
    h = h_ref[...] + _dot(y, w_out_ref[...])
    h = _ple_update(h, p_ref[...], ple_gain_ref[...], w_gate_ref[...], w_ple_ref[...])
    out_ref[...] = _rms(h, fin_gain_ref[...])


def _final(h, o, g, p, ple_gain, fin_gain, w_out, w_gate, w_ple):
    n_tok = h.shape[0]
    tm = TM_PROJ
    row = lambda w: pl.BlockSpec((tm, w), lambda i: (i, 0))
    sq = _const_spec((D_MODEL, D_MODEL))
    vec = _const_spec((1, D_MODEL))
    return pl.pallas_call(
        _final_kernel,
        grid=(n_tok // tm,),
        in_specs=[row(D_MODEL), row(D_MODEL), row(D_MODEL), row(PLE_DIM), vec, vec,
                  sq, sq, _const_spec((PLE_DIM, D_MODEL))],
        out_specs=row(D_MODEL),
        out_shape=jax.ShapeDtypeStruct((n_tok, D_MODEL), F32),
        compiler_params=pltpu.CompilerParams(dimension_semantics=("arbitrary",),
                                             vmem_limit_bytes=VMEM_LIMIT),
        name="final",
    )(h, o, g, p, ple_gain, fin_gain, w_out, w_gate, w_ple)


def _interleave_heads(w_a, w_b, heads, dim):
    d_in = w_a.shape[0]
    stacked = jnp.stack([w_a.reshape(d_in, heads, dim), w_b.reshape(d_in, heads, dim)], axis=2)
    return stacked.reshape(d_in, heads * 2 * dim)


def kernel(x, p, attn_norm, w_in_a, w_out_a, kv_norm, w_kv, w_q_b, diff_lambda, diff_subln, w_out_b, rel_bias,
           ple_norm, w_ple_proj, w_ple_gate, final_norm):
    batch, seq, d = x.shape
    depth = p.shape[0]
    assert d == D_MODEL and depth == 2 and w_in_a.shape[0] == 1 and w_q_b.shape[0] == 1
    assert seq % T == 0 and (batch * seq) % TM_PROJ == 0
    n_tok = batch * seq
    x2 = x.reshape(n_tok, d)
    p2 = p.reshape(depth, n_tok, PLE_DIM)
    vec = lambda v: v.reshape(1, -1).astype(F32)
    qk_width = DIFF_HEADS * DIFF_QK_DIM

    wq, wk, wv, wg = jnp.split(w_in_a[0], 4, axis=-1)
    wkg_a = jnp.concatenate([wk, wg], axis=1).astype(BF16)
    wqvt_a = jnp.concatenate([wq, wv], axis=1).T.astype(BF16)
    wk1, wk2, wvv = jnp.split(w_kv, [qk_width, 2 * qk_width], axis=-1)
    wq1, wq2, wg_b = jnp.split(w_q_b[0], [qk_width, 2 * qk_width], axis=-1)
    w_k_b = _interleave_heads(wk1, wk2, DIFF_HEADS, DIFF_QK_DIM).astype(BF16)
    w_qt_b = _interleave_heads(wq1, wq2, DIFF_HEADS, DIFF_QK_DIM).T.astype(BF16)
    w_vt_b = wvv.T.astype(BF16)

    lam_init = 0.8 - 0.6 * math.exp(-0.3 * 1)
    bias, lam = _bias_lam(rel_bias.astype(F32), diff_lambda[0].astype(F32), lam_init)

    k_a, g_a, qt_a, vt_a = _inproj_a(x2, vec(attn_norm[0]), wkg_a, wqvt_a, 1.0 / math.sqrt(SB_HEAD_DIM))
    o_a = _sb_attn(qt_a, k_a, vt_a, batch, seq)

    h1, k_b, vt_b, qt_b, g_b = _mid(
        x2, o_a, g_a, p2[0], vec(ple_norm[0]), vec(kv_norm), vec(attn_norm[1]),
        w_out_a[0].astype(BF16), w_ple_gate[0].astype(BF16), w_ple_proj[0].astype(BF16),
        w_k_b, w_vt_b, w_qt_b, wg_b.astype(BF16), 1.0 / math.sqrt(DIFF_QK_DIM))

    o_b = _diff_attn(lam[0:1, 0:1], qt_b, k_b, vt_b, bias, diff_subln[0].reshape(DIFF_V_DIM, 1).astype(F32),
                     batch, seq, 1.0 - lam_init)

    out = _final(h1, o_b, g_b, p2[1], vec(ple_norm[1]), vec(final_norm),
                 w_out_b[0].astype(BF16), w_ple_gate[1].astype(BF16), w_ple_proj[1].astype(BF16))
    return out.reshape(batch, seq, d)
```

```python
import functools
import math

import jax
import jax.numpy as jnp
from jax import lax
from jax.experimental import pallas as pl
from jax.experimental.pallas import tpu as pltpu

D_MODEL = 1024
PLE_DIM = 256
SB_HEAD_DIM = 64
DIFF_QK_DIM = 64
DIFF_V_DIM = 128
DIFF_HEADS = 8
N_BUCKETS = 32
MAX_DISTANCE = 128
EPS = 1e-6
NEG_INF = -1e30

LANES = 128
T = 256
SB_GROUP = 4
TM_PROJ = 512
TM_MID = 256
VMEM_LIMIT = 56 * 1024 * 1024

F32 = jnp.float32
BF16 = jnp.bfloat16
NT_DIMS = (((1,), (1,)), ((), ()))


def _rms(x, gain):
    ms = jnp.mean(x * x, axis=-1, keepdims=True)
    return x * lax.rsqrt(ms + EPS) * gain


def _silu(g):
    return g * jax.nn.sigmoid(g)


def _dot(a, b):
    return jnp.dot(a, b, preferred_element_type=F32)


def _dot_nt(a, b):
    return lax.dot_general(a, b, NT_DIMS, preferred_element_type=F32)


def _store_token_major_t(ref, val_t, scale=None):
    for c in range(val_t.shape[1] // T):
        blk = val_t[:, c * T:(c + 1) * T]
        if scale is not None:
            blk = blk * scale
        ref[c] = blk.astype(BF16)


def _bias_lam_kernel(table_ref, lp_ref, bias_ref, lam_ref, *, lam_init):
    h = pl.program_id(0)
    key = lax.broadcasted_iota(jnp.int32, (T, T), 0)
    qry = lax.broadcasted_iota(jnp.int32, (T, T), 1)
    max_exact = N_BUCKETS // 2
    far = table_ref[N_BUCKETS - 1, h]
    for d in range(2):
        dist = qry - key + d * T
        n = jnp.maximum(dist, 0)
        nf = jnp.maximum(n, 1).astype(F32)
        large = max_exact + (jnp.log(nf / max_exact) / math.log(MAX_DISTANCE / max_exact)
                             * (N_BUCKETS - max_exact)).astype(jnp.int32)
        large = jnp.minimum(large, N_BUCKETS - 1)
        bucket = jnp.where(n < max_exact, n, large)
        val = jnp.zeros((T, T), F32)
        for b in range(N_BUCKETS):
            val = jnp.where(bucket == b, table_ref[b, h], val)
        bias_ref[0, d] = jnp.where(dist >= 0, val - far, NEG_INF)

    lp = lp_ref[...]
    s1 = jnp.sum(lp[0:1, :] * lp[1:2, :], axis=-1, keepdims=True)
    s2 = jnp.sum(lp[2:3, :] * lp[3:4, :], axis=-1, keepdims=True)
    lam = jnp.exp(s1) - jnp.exp(s2) + lam_init
    lam_ref[...] = jnp.broadcast_to(lam, lam_ref.shape)


def _bias_lam(rel_bias, lp, lam_init):
    return pl.pallas_call(
        functools.partial(_bias_lam_kernel, lam_init=lam_init),
        grid=(DIFF_HEADS,),
        in_specs=[pl.BlockSpec(memory_space=pltpu.SMEM),
                  pl.BlockSpec((4, DIFF_QK_DIM), lambda h: (0, 0))],
        out_specs=[pl.BlockSpec((1, 2, T, T), lambda h: (h, 0, 0, 0)),
                   pl.BlockSpec((8, LANES), lambda h: (0, 0))],
        out_shape=[jax.ShapeDtypeStruct((DIFF_HEADS, 2, T, T), F32),
                   jax.ShapeDtypeStruct((8, LANES), F32)],
        name="bias_lam",
    )(rel_bias, lp)


def _inproj_a_kernel(x_ref, gain_ref, wkg_ref, wqvt_ref, k_ref, g_ref, qt_ref, vt_ref, *, scale):
    hn = _rms(x_ref[...], gain_ref[...]).astype(BF16)
    k_ref[...] = _dot(hn, wkg_ref[:, :D_MODEL]).astype(BF16)
    g_ref[...] = _dot(hn, wkg_ref[:, D_MODEL:])
    _store_token_major_t(qt_ref, _dot_nt(wqvt_ref[:D_MODEL, :], hn), scale)
    _store_token_major_t(vt_ref, _dot_nt(wqvt_ref[D_MODEL:, :], hn))


def _const_spec(shape):
    return pl.BlockSpec(shape, lambda *_: (0,) * len(shape))


def _inproj_a(x2, gain, wkg, wqvt, scale):
    n_tok = x2.shape[0]
    tm = TM_PROJ
    return pl.pallas_call(
        functools.partial(_inproj_a_kernel, scale=scale),
        grid=(n_tok // tm,),
        in_specs=[pl.BlockSpec((tm, D_MODEL), lambda i: (i, 0)),
                  _const_spec((1, D_MODEL)),
                  _const_spec((D_MODEL, 2 * D_MODEL)),
                  _const_spec((2 * D_MODEL, D_MODEL))],
        out_specs=[pl.BlockSpec((tm, D_MODEL), lambda i: (i, 0)),
                   pl.BlockSpec((tm, D_MODEL), lambda i: (i, 0)),
                   pl.BlockSpec((tm // T, D_MODEL, T), lambda i: (i, 0, 0)),
                   pl.BlockSpec((tm // T, D_MODEL, T), lambda i: (i, 0, 0))],
        out_shape=[jax.ShapeDtypeStruct((n_tok, D_MODEL), BF16),
                   jax.ShapeDtypeStruct((n_tok, D_MODEL), F32),
                   jax.ShapeDtypeStruct((n_tok // T, D_MODEL, T), BF16),
                   jax.ShapeDtypeStruct((n_tok // T, D_MODEL, T), BF16)],
        compiler_params=pltpu.CompilerParams(dimension_semantics=("arbitrary",),
                                             vmem_limit_bytes=VMEM_LIMIT),
        name="inproj_a",
    )(x2, gain, wkg, wqvt)


def _neg_abs(x):
    bits = lax.bitcast_convert_type(x, jnp.uint32) | jnp.uint32(0x80000000)
    return lax.bitcast_convert_type(bits, F32)


def _sb_attn_kernel(qt_ref, k_ref, vt_ref, o_ref,
                    z_s, sp_s, ls_s, w_s, acc_s, run_s, scale_s, sp0_s):
    i = pl.program_id(2)
    n = i + 1
    hd = SB_HEAD_DIM
    heads = range(SB_GROUP)
    row = lax.broadcasted_iota(jnp.int32, (LANES, T), 0)
    q_heads = []
    for hh in heads:
        g, half = divmod(hh, 2)
        qg = qt_ref[0, g * LANES:(g + 1) * LANES, :]
        keep = (row < hd) if half == 0 else (row >= hd)
        q_heads.append(jnp.where(keep, qg, jnp.zeros_like(qg)))
    key = lax.broadcasted_iota(jnp.int32, (T, T), 0)
    qry = lax.broadcasted_iota(jnp.int32, (T, T), 1)
    upper = jnp.where(qry > key, 1.0, 0.0).astype(BF16)
    past = key < qry

    def stage_p(t):
        j = i - (t - 3)
        for hh in heads:
            vb = vt_ref[j, hh * hd:(hh + 1) * hd, :]
            acc_s[hh] = acc_s[hh] + _dot(vb, w_s[hh]) * scale_s[hh]

    def stage_c():
        return [_dot(upper, sp_s[hh]) for hh in heads]

    def stage_e(t, c, diag):
        slot = (t - 2) & 1
        for hh in heads:
            w = jnp.exp(ls_s[slot, hh] - c[hh])
            if diag:
                w = jnp.where(past, w, 0.0)
            w_s[hh] = w.astype(BF16)
            run = run_s[hh]
            scale_s[hh] = jnp.exp(-run)
            run_s[hh] = run + c[hh][0:1, :] + sp0_s[hh]

    def stage_s(t, diag):
        slot = (t - 1) & 1
        sp0 = []
        for hh in heads:
            z = z_s[hh]
            soft = jnp.log(1.0 + jnp.exp(_neg_abs(z)))
            sp = jnp.maximum(z, 0.0) + soft
            ls_s[slot, hh] = z - sp
            if diag:
                sp = jnp.where(past, sp, 0.0)
            sp_s[hh] = sp.astype(BF16)
            sp0.append(sp[0:1, :])
        return sp0

    def stage_z(t):
        j = jnp.maximum(i - t, 0)
        start = pl.multiple_of(j * T, T)
        for hh in heads:
            g = hh // 2
            z_s[hh] = _dot(k_ref[pl.ds(start, T), g * LANES:(g + 1) * LANES], q_heads[hh])

    def step(t, first, last, e_diag=False, s_diag=False):
        if first <= 3 <= last:
            stage_p(t)
        c = stage_c() if first <= 2 <= last else None
        sp0 = stage_s(t, s_diag) if first <= 1 <= last else None
        if c is not None:
            stage_e(t, c, e_diag)
        if sp0 is not None:
            for hh in heads:
                sp0_s[hh] = sp0[hh]
        if first <= 0 <= last:
            stage_z(t)

    for hh in heads:
        acc_s[hh] = jnp.zeros((hd, T), F32)
        run_s[hh] = jnp.zeros((1, T), F32)

    step(0, 0, 0)
    step(1, 0, 1, s_diag=True)
    step(2, 0, 2, e_diag=True)

    def steady(t, carry):
        step(t, 0, 3)
        return carry

    lax.fori_loop(3, n, steady, 0)

    for d in range(2):
        @pl.when(n + d >= 3)
        def _():
            step(n + d, d + 1, 3)

    step(n + 2, 3, 3)

    o_t = jnp.concatenate([acc_s[hh] for hh in heads], axis=0)
    o_ref[...] = o_t.T


def _sb_attn(qt, k, vt, batch, seq):
    nq = seq // T
    n_tok = batch * seq
    width = SB_GROUP * SB_HEAD_DIM
    groups = D_MODEL // width
    tile_f32 = pltpu.VMEM((SB_GROUP, T, T), F32)
    tile_bf16 = pltpu.VMEM((SB_GROUP, T, T), BF16)
    row_f32 = pltpu.VMEM((SB_GROUP, 1, T), F32)
    return pl.pallas_call(
        _sb_attn_kernel,
        grid=(batch, groups, nq),
        in_specs=[pl.BlockSpec((1, width, T), lambda b, h, i: (b * nq + i, h, 0)),
                  pl.BlockSpec((seq, width), lambda b, h, i: (b, h)),
                  pl.BlockSpec((nq, width, T), lambda b, h, i: (b, h, 0))],
        out_specs=pl.BlockSpec((T, width), lambda b, h, i: (b * nq + i, h)),
        out_shape=jax.ShapeDtypeStruct((n_tok, D_MODEL), F32),
        scratch_shapes=[tile_f32,
                        tile_bf16,
                        pltpu.VMEM((2, SB_GROUP, T, T), F32),
                        tile_bf16,
                        pltpu.VMEM((SB_GROUP, SB_HEAD_DIM, T), F32),
                        row_f32, row_f32, row_f32],
        compiler_params=pltpu.CompilerParams(dimension_semantics=("arbitrary",) * 3,
                                             vmem_limit_bytes=VMEM_LIMIT),
        name="sb_attn",
    )(qt, k, vt)


def _ple_update(h, p, ple_gain, w_gate, w_ple):
    gate = jax.nn.sigmoid(_dot(_rms(h, ple_gain).astype(BF16), w_gate))
    return h + gate * _dot(p.astype(BF16), w_ple)


def _mid_kernel(x_ref, o_ref, g_ref, p_ref, ple_gain_ref, kv_gain_ref, q_gain_ref,
                w_out_ref, w_gate_ref, w_ple_ref, w_k_ref, w_vt_ref, w_qt_ref, w_g_ref,
                h_ref, k_ref, vt_ref, qt_ref, g2_ref, *, scale):
    y = (o_ref[...] * _silu(g_ref[...])).astype(BF16)
    h = x_ref[...] + _dot(y, w_out_ref[...])
    h = _ple_update(h, p_ref[...], ple_gain_ref[...], w_gate_ref[...], w_ple_ref[...])
    h_ref[...] = h
    hk = _rms(h, kv_gain_ref[...]).astype(BF16)
    k_ref[...] = _dot(hk, w_k_ref[...]).astype(BF16)
    _store_token_major_t(vt_ref, _dot_nt(w_vt_ref[...], hk))
    hq = _rms(h, q_gain_ref[...]).astype(BF16)
    _store_token_major_t(qt_ref, _dot_nt(w_qt_ref[...], hq), scale)
    g2_ref[...] = _dot(hq, w_g_ref[...])


def _mid(x2, o, g, p, ple_gain, kv_gain, q_gain, w_out, w_gate, w_ple, w_k, w_vt, w_qt, w_g, scale):
    n_tok = x2.shape[0]
    tm = TM_MID
    row = lambda w: pl.BlockSpec((tm, w), lambda i: (i, 0))
    sq = _const_spec((D_MODEL, D_MODEL))
    vec = _const_spec((1, D_MODEL))
    tmaj = pl.BlockSpec((tm // T, D_MODEL, T), lambda i: (i, 0, 0))
    return pl.pallas_call(
        functools.partial(_mid_kernel, scale=scale),
        grid=(n_tok // tm,),
        in_specs=[row(D_MODEL), row(D_MODEL), row(D_MODEL), row(PLE_DIM), vec, vec, vec,
                  sq, sq, _const_spec((PLE_DIM, D_MODEL)), sq, sq, sq, sq],
        out_specs=[row(D_MODEL), row(D_MODEL), tmaj, tmaj, row(D_MODEL)],
        out_shape=[jax.ShapeDtypeStruct((n_tok, D_MODEL), F32),
                   jax.ShapeDtypeStruct((n_tok, D_MODEL), BF16),
                   jax.ShapeDtypeStruct((n_tok // T, D_MODEL, T), BF16),
                   jax.ShapeDtypeStruct((n_tok // T, D_MODEL, T), BF16),
                   jax.ShapeDtypeStruct((n_tok, D_MODEL), F32)],
        compiler_params=pltpu.CompilerParams(dimension_semantics=("arbitrary",),
                                             vmem_limit_bytes=VMEM_LIMIT),
        name="mid",
    )(x2, o, g, p, ple_gain, kv_gain, q_gain, w_out, w_gate, w_ple, w_k, w_vt, w_qt, w_g)


def _diff_attn_kernel(lam_ref, qt_ref, k_ref, vt_ref, bias_ref, subln_ref, o_ref, *, out_scale):
    i = pl.program_id(2)
    dk = DIFF_QK_DIM
    qt = qt_ref[0]
    row = lax.broadcasted_iota(jnp.int32, qt.shape, 0)
    zero = jnp.zeros_like(qt)
    q_maps = (jnp.where(row < dk, qt, zero), jnp.where(row >= dk, qt, zero))

    def tile(j, carry, bias):
        start = pl.multiple_of(j * T, T)
        kb = k_ref[pl.ds(start, T), :]
        vb = vt_ref[j]
        out = []
        for mm in range(2):
            m, l, acc = carry[mm]
            s = _dot(kb, q_maps[mm])
            if bias is not None:
                s = s + bias
            m_new = jnp.maximum(m, jnp.max(s, axis=0, keepdims=True))
            alpha = jnp.exp(m - m_new)
            p = jnp.exp(s - m_new)
            l = l * alpha + jnp.sum(p, axis=0, keepdims=True)
            acc = acc * alpha + _dot(vb, p.astype(BF16))
            out.append((m_new, l, acc))
        return tuple(out)

    init = tuple((jnp.full((1, T), NEG_INF, F32), jnp.zeros((1, T), F32), jnp.zeros((DIFF_V_DIM, T), F32))
                 for _ in range(2))
    carry = tile(i, init, bias_ref[0, 0])
    near = jnp.minimum(i, 1)
    carry = lax.fori_loop(0, near, lambda t, c: tile(i - 1, c, bias_ref[0, 1]), carry)
    carry = lax.fori_loop(0, i - near, lambda t, c: tile(i - 2 - t, c, None), carry)

    (_, l1, acc1), (_, l2, acc2) = carry
    lam = lam_ref[0, 0]
    o = acc1 * (1.0 / l1) - lam * (acc2 * (1.0 / l2))
    ms = jnp.mean(o * o, axis=0, keepdims=True)
    y = o * lax.rsqrt(ms + EPS) * subln_ref[...] * out_scale
    o_ref[...] = y.T


def _diff_attn(lam, qt, k, vt, bias, subln, batch, seq, out_scale):
    nq = seq // T
    n_tok = batch * seq
    return pl.pallas_call(
        functools.partial(_diff_attn_kernel, out_scale=out_scale),
        grid=(batch, DIFF_HEADS, nq),
        in_specs=[pl.BlockSpec(memory_space=pltpu.SMEM),
                  pl.BlockSpec((1, LANES, T), lambda b, h, i: (b * nq + i, h, 0)),
                  pl.BlockSpec((seq, LANES), lambda b, h, i: (b, h)),
                  pl.BlockSpec((nq, LANES, T), lambda b, h, i: (b, h, 0)),
                  pl.BlockSpec((1, 2, T, T), lambda b, h, i: (h, 0, 0, 0)),
                  pl.BlockSpec((DIFF_V_DIM, 1), lambda b, h, i: (0, 0))],
        out_specs=pl.BlockSpec((T, LANES), lambda b, h, i: (b * nq + i, h)),
        out_shape=jax.ShapeDtypeStruct((n_tok, D_MODEL), F32),
        compiler_params=pltpu.CompilerParams(dimension_semantics=("arbitrary",) * 3,
                                             vmem_limit_bytes=VMEM_LIMIT),
        name="diff_attn",
    )(lam, qt, k, vt, bias, subln)


def _final_kernel(h_ref, o_ref, g_ref, p_ref, ple_gain_ref, fin_gain_ref,
                  w_out_ref, w_gate_ref, w_ple_ref, out_ref):
    y = (o_ref[...] * _silu(g_ref[...])).astype(BF16)
    h = h_ref[...] + _dot(y, w_out_ref[...])
    h = _ple_update(h, p_ref[...], ple_gain_ref[...], w_gate_ref[...], w_ple_ref[...])
    out_ref[...] = _rms(h, fin_gain_ref[...])


def _final(h, o, g, p, ple_gain, fin_gain, w_out, w_gate, w_ple):
    n_tok = h.shape[0]
    tm = TM_PROJ
    row = lambda w: pl.BlockSpec((tm, w), lambda i: (i, 0))
    sq = _const_spec((D_MODEL, D_MODEL))
    vec = _const_spec((1, D_MODEL))
    return pl.pallas_call(
        _final_kernel,
        grid=(n_tok // tm,),
        in_specs=[row(D_MODEL), row(D_MODEL), row(D_MODEL), row(PLE_DIM), vec, vec,
                  sq, sq, _const_spec((PLE_DIM, D_MODEL))],
        out_specs=row(D_MODEL),
        out_shape=jax.ShapeDtypeStruct((n_tok, D_MODEL), F32),
        compiler_params=pltpu.CompilerParams(dimension_semantics=("arbitrary",),
                                             vmem_limit_bytes=VMEM_LIMIT),
        name="final",
    )(h, o, g, p, ple_gain, fin_gain, w_out, w_gate, w_ple)


def _interleave_heads(w_a, w_b, heads, dim):
    d_in = w_a.shape[0]
    stacked = jnp.stack([w_a.reshape(d_in, heads, dim), w_b.reshape(d_in, heads, dim)], axis=2)
    return stacked.reshape(d_in, heads * 2 * dim)


def kernel(x, p, attn_norm, w_in_a, w_out_a, kv_norm, w_kv, w_q_b, diff_lambda, diff_subln, w_out_b, rel_bias,
           ple_norm, w_ple_proj, w_ple_gate, final_norm):
    batch, seq, d = x.shape
    depth = p.shape[0]
    assert d == D_MODEL and depth == 2 and w_in_a.shape[0] == 1 and w_q_b.shape[0] == 1
    assert seq % T == 0 and (batch * seq) % TM_PROJ == 0
    n_tok = batch * seq
    x2 = x.reshape(n_tok, d)
    p2 = p.reshape(depth, n_tok, PLE_DIM)
    vec = lambda v: v.reshape(1, -1).astype(F32)
    qk_width = DIFF_HEADS * DIFF_QK_DIM

    wq, wk, wv, wg = jnp.split(w_in_a[0], 4, axis=-1)
    wkg_a = jnp.concatenate([wk, wg], axis=1).astype(BF16)
    wqvt_a = jnp.concatenate([wq, wv], axis=1).T.astype(BF16)
    wk1, wk2, wvv = jnp.split(w_kv, [qk_width, 2 * qk_width], axis=-1)
    wq1, wq2, wg_b = jnp.split(w_q_b[0], [qk_width, 2 * qk_width], axis=-1)
    w_k_b = _interleave_heads(wk1, wk2, DIFF_HEADS, DIFF_QK_DIM).astype(BF16)
    w_qt_b = _interleave_heads(wq1, wq2, DIFF_HEADS, DIFF_QK_DIM).T.astype(BF16)
    w_vt_b = wvv.T.astype(BF16)

    lam_init = 0.8 - 0.6 * math.exp(-0.3 * 1)
    bias, lam = _bias_lam(rel_bias.astype(F32), diff_lambda[0].astype(F32), lam_init)

    k_a, g_a, qt_a, vt_a = _inproj_a(x2, vec(attn_norm[0]), wkg_a, wqvt_a, 1.0 / math.sqrt(SB_HEAD_DIM))
    o_a = _sb_attn(qt_a, k_a, vt_a, batch, seq)

    h1, k_b, vt_b, qt_b, g_b = _mid(
        x2, o_a, g_a, p2[0], vec(ple_norm[0]), vec(kv_norm), vec(attn_norm[1]),
        w_out_a[0].astype(BF16), w_ple_gate[0].astype(BF16), w_ple_proj[0].astype(BF16),
        w_k_b, w_vt_b, w_qt_b, wg_b.astype(BF16), 1.0 / math.sqrt(DIFF_QK_DIM))

    o_b = _diff_attn(lam[0:1, 0:1], qt_b, k_b, vt_b, bias, diff_subln[0].reshape(DIFF_V_DIM, 1).astype(F32),
                     batch, seq, 1.0 - lam_init)

    out = _final(h1, o_b, g_b, p2[1], vec(ple_norm[1]), vec(final_norm),
                 w_out_b[0].astype(BF16), w_ple_gate[1].astype(BF16), w_ple_proj[1].astype(BF16))
    return out.reshape(batch, seq, d)
```

```python
import functools
import math

import jax
import jax.numpy as jnp
from jax import lax
from jax.experimental import pallas as pl
from jax.experimental.pallas import tpu as pltpu

D_MODEL = 1024
PLE_DIM = 256
SB_HEAD_DIM = 64
DIFF_QK_DIM = 64
DIFF_V_DIM = 128
DIFF_HEADS = 8
N_BUCKETS = 32
MAX_DISTANCE = 128
EPS = 1e-6
NEG_INF = -1e30

LANES = 128
T = 256
SB_GROUP = 4
DIFF_GROUP = 2
TM_PROJ = 512
TM_MID = 256
VMEM_LIMIT = 56 * 1024 * 1024

F32 = jnp.float32
BF16 = jnp.bfloat16
NT_DIMS = (((1,), (1,)), ((), ()))


def _rms(x, gain):
    ms = jnp.mean(x * x, axis=-1, keepdims=True)
    return x * lax.rsqrt(ms + EPS) * gain


def _silu(g):
    return g * jax.nn.sigmoid(g)


def _dot(a, b):
    return jnp.dot(a, b, preferred_element_type=F32)


def _dot_nt(a, b):
    return lax.dot_general(a, b, NT_DIMS, preferred_element_type=F32)


def _store_token_major_t(ref, val_t, scale=None):
    for c in range(val_t.shape[1] // T):
        blk = val_t[:, c * T:(c + 1) * T]
        if scale is not None:
            blk = blk * scale
        ref[c] = blk.astype(BF16)


def _const_spec(shape):
    return pl.BlockSpec(shape, lambda *_: (0,) * len(shape))


def _pipeline(n, n_stages, peel, step):
    last = n_stages - 1
    for t in range(peel):
        step(t, 0, min(t, last), t % 2)

    def pair(tt, carry):
        t = peel + 2 * tt
        step(t, 0, last, peel % 2)
        step(t + 1, 0, last, 1 - peel % 2)
        return carry

    steady = jnp.maximum(n - peel, 0)
    lax.fori_loop(0, steady // 2, pair, 0)

    @pl.when(steady % 2 == 1)
    def _():
        step(n - 1, 0, last, peel % 2)

    for d in range(last):
        @pl.when(n + d >= peel)
        def _():
            step(n + d, d + 1, last, None)


def _bias_lam_kernel(table_ref, lp_ref, bias_ref, lam_ref, *, lam_init):
    h = pl.program_id(0)
    key = lax.broadcasted_iota(jnp.int32, (T, T), 0)
    qry = lax.broadcasted_iota(jnp.int32, (T, T), 1)
    max_exact = N_BUCKETS // 2
    far = table_ref[N_BUCKETS - 1, h]
    for d in range(2):
        dist = qry - key + d * T
        n = jnp.maximum(dist, 0)
        nf = jnp.maximum(n, 1).astype(F32)
        large = max_exact + (jnp.log(nf / max_exact) / math.log(MAX_DISTANCE / max_exact)
                             * (N_BUCKETS - max_exact)).astype(jnp.int32)
        large = jnp.minimum(large, N_BUCKETS - 1)
        bucket = jnp.where(n < max_exact, n, large)
        val = jnp.zeros((T, T), F32)
        for b in range(N_BUCKETS):
            val = jnp.where(bucket == b, table_ref[b, h], val)
        bias_ref[0, d] = jnp.where(dist >= 0, val - far, NEG_INF)

    lp = lp_ref[...]
    s1 = jnp.sum(lp[0:1, :] * lp[1:2, :], axis=-1, keepdims=True)
    s2 = jnp.sum(lp[2:3, :] * lp[3:4, :], axis=-1, keepdims=True)
    lam = jnp.exp(s1) - jnp.exp(s2) + lam_init
    lam_ref[...] = jnp.broadcast_to(lam, lam_ref.shape)


def _bias_lam(rel_bias, lp, lam_init):
    return pl.pallas_call(
        functools.partial(_bias_lam_kernel, lam_init=lam_init),
        grid=(DIFF_HEADS,),
        in_specs=[pl.BlockSpec(memory_space=pltpu.SMEM),
                  pl.BlockSpec((4, DIFF_QK_DIM), lambda h: (0, 0))],
        out_specs=[pl.BlockSpec((1, 2, T, T), lambda h: (h, 0, 0, 0)),
                   pl.BlockSpec((8, LANES), lambda h: (0, 0))],
        out_shape=[jax.ShapeDtypeStruct((DIFF_HEADS, 2, T, T), F32),
                   jax.ShapeDtypeStruct((8, LANES), F32)],
        name="bias_lam",
    )(rel_bias, lp)


def _inproj_a_kernel(x_ref, gain_ref, wkg_ref, wqvt_ref, k_ref, g_ref, qt_ref, vt_ref, *, scale):
    hn = _rms(x_ref[...], gain_ref[...]).astype(BF16)
    k_ref[...] = _dot(hn, wkg_ref[:, :D_MODEL]).astype(BF16)
    g_ref[...] = _dot(hn, wkg_ref[:, D_MODEL:])
    _store_token_major_t(qt_ref, _dot_nt(wqvt_ref[:D_MODEL, :], hn), scale)
    _store_token_major_t(vt_ref, _dot_nt(wqvt_ref[D_MODEL:, :], hn))


def _inproj_a(x2, gain, wkg, wqvt, scale):
    n_tok = x2.shape[0]
    tm = TM_PROJ
    return pl.pallas_call(
        functools.partial(_inproj_a_kernel, scale=scale),
        grid=(n_tok // tm,),
        in_specs=[pl.BlockSpec((tm, D_MODEL), lambda i: (i, 0)),
                  _const_spec((1, D_MODEL)),
                  _const_spec((D_MODEL, 2 * D_MODEL)),
                  _const_spec((2 * D_MODEL, D_MODEL))],
        out_specs=[pl.BlockSpec((tm, D_MODEL), lambda i: (i, 0)),
                   pl.BlockSpec((tm, D_MODEL), lambda i: (i, 0)),
                   pl.BlockSpec((tm // T, D_MODEL, T), lambda i: (i, 0, 0)),
                   pl.BlockSpec((tm // T, D_MODEL, T), lambda i: (i, 0, 0))],
        out_shape=[jax.ShapeDtypeStruct((n_tok, D_MODEL), BF16),
                   jax.ShapeDtypeStruct((n_tok, D_MODEL), F32),
                   jax.ShapeDtypeStruct((n_tok // T, D_MODEL, T), BF16),
                   jax.ShapeDtypeStruct((n_tok // T, D_MODEL, T), BF16)],
        compiler_params=pltpu.CompilerParams(dimension_semantics=("arbitrary",),
                                             vmem_limit_bytes=VMEM_LIMIT),
        name="inproj_a",
    )(x2, gain, wkg, wqvt)


def _neg_abs(x):
    bits = lax.bitcast_convert_type(x, jnp.uint32) | jnp.uint32(0x80000000)
    return lax.bitcast_convert_type(bits, F32)


def _sb_attn_kernel(qt_ref, k_ref, vt_ref, o_ref,
                    z_s, sp_s, ls_s, w_s, acc_s, run_s, scale_s, sp0_s):
    i = pl.program_id(2)
    n = i + 1
    hd = SB_HEAD_DIM
    heads = range(SB_GROUP)
    row = lax.broadcasted_iota(jnp.int32, (LANES, T), 0)
    q_heads = []
    for hh in heads:
        g, half = divmod(hh, 2)
        qg = qt_ref[0, g * LANES:(g + 1) * LANES, :]
        keep = (row < hd) if half == 0 else (row >= hd)
        q_heads.append(jnp.where(keep, qg, jnp.zeros_like(qg)))
    key = lax.broadcasted_iota(jnp.int32, (T, T), 0)
    qry = lax.broadcasted_iota(jnp.int32, (T, T), 1)
    upper = jnp.where(qry > key, 1.0, 0.0).astype(BF16)
    past = key < qry

    def stage_p(t):
        j = i - (t - 3)
        for hh in heads:
            vb = vt_ref[j, hh * hd:(hh + 1) * hd, :]
            acc_s[hh] = acc_s[hh] + _dot(vb, w_s[hh]) * scale_s[hh]

    def stage_c():
        return [_dot(upper, sp_s[hh]) for hh in heads]

    def stage_e(c, slot, diag):
        for hh in heads:
            w = jnp.exp(ls_s[slot, hh] - c[hh])
            if diag:
                w = jnp.where(past, w, 0.0)
            w_s[hh] = w.astype(BF16)
            run = run_s[hh]
            scale_s[hh] = jnp.exp(-run)
            run_s[hh] = run + c[hh][0:1, :] + sp0_s[hh]

    def stage_s(slot, diag):
        sp0 = []
        for hh in heads:
            z = z_s[slot, hh]
            soft = jnp.log(1.0 + jnp.exp(_neg_abs(z)))
            sp = jnp.maximum(z, 0.0) + soft
            ls_s[slot, hh] = z - sp
            if diag:
                sp = jnp.where(past, sp, 0.0)
            sp_s[hh] = sp.astype(BF16)
            sp0.append(sp[0:1, :])
        return sp0

    def stage_z(t, slot):
        j = jnp.maximum(i - t, 0)
        start = pl.multiple_of(j * T, T)
        for hh in heads:
            g = hh // 2
            z_s[slot, hh] = _dot(k_ref[pl.ds(start, T), g * LANES:(g + 1) * LANES], q_heads[hh])

    def step(t, first, last, parity):
        now = (t & 1) if parity is None else parity
        prev = 1 - now
        peeled = isinstance(t, int)
        if first <= 3 <= last:
            stage_p(t)
        c = stage_c() if first <= 2 <= last else None
        if first <= 0 <= last:
            stage_z(t, now)
        sp0 = stage_s(prev, diag=(peeled and t == 1)) if first <= 1 <= last else None
        if c is not None:
            stage_e(c, now, diag=(peeled and t == 2))
        if sp0 is not None:
            for hh in heads:
                sp0_s[hh] = sp0[hh]

    for hh in heads:
        acc_s[hh] = jnp.zeros((hd, T), F32)
        run_s[hh] = jnp.zeros((1, T), F32)

    _pipeline(n, 4, 3, step)

    o_t = jnp.concatenate([acc_s[hh] for hh in heads], axis=0)
    o_ref[...] = o_t.T


def _sb_attn(qt, k, vt, batch, seq):
    nq = seq // T
    n_tok = batch * seq
    width = SB_GROUP * SB_HEAD_DIM
    groups = D_MODEL // width
    tile_bf16 = pltpu.VMEM((SB_GROUP, T, T), BF16)
    pingpong_f32 = pltpu.VMEM((2, SB_GROUP, T, T), F32)
    row_f32 = pltpu.VMEM((SB_GROUP, 1, T), F32)
    return pl.pallas_call(
        _sb_attn_kernel,
        grid=(batch, groups, nq),
        in_specs=[pl.BlockSpec((1, width, T), lambda b, h, i: (b * nq + i, h, 0)),
                  pl.BlockSpec((seq, width), lambda b, h, i: (b, h)),
                  pl.BlockSpec((nq, width, T), lambda b, h, i: (b, h, 0))],
        out_specs=pl.BlockSpec((T, width), lambda b, h, i: (b * nq + i, h)),
        out_shape=jax.ShapeDtypeStruct((n_tok, D_MODEL), F32),
        scratch_shapes=[pingpong_f32,
                        tile_bf16,
                        pingpong_f32,
                        tile_bf16,
                        pltpu.VMEM((SB_GROUP, SB_HEAD_DIM, T), F32),
                        row_f32, row_f32, row_f32],
        compiler_params=pltpu.CompilerParams(dimension_semantics=("arbitrary",) * 3,
                                             vmem_limit_bytes=VMEM_LIMIT),
        name="sb_attn",
    )(qt, k, vt)


def _ple_update(h, p, ple_gain, w_gate, w_ple):
    gate = jax.nn.sigmoid(_dot(_rms(h, ple_gain).astype(BF16), w_gate))
    return h + gate * _dot(p.astype(BF16), w_ple)


def _mid_kernel(x_ref, o_ref, g_ref, p_ref, ple_gain_ref, kv_gain_ref, q_gain_ref,
                w_out_ref, w_gate_ref, w_ple_ref, w_k_ref, w_vt_ref, w_qt_ref, w_g_ref,
                h_ref, k_ref, vt_ref, qt_ref, g2_ref, *, scale):
    y = (o_ref[...] * _silu(g_ref[...])).astype(BF16)
    h = x_ref[...] + _dot(y, w_out_ref[...])
    h = _ple_update(h, p_ref[...], ple_gain_ref[...], w_gate_ref[...], w_ple_ref[...])
    h_ref[...] = h
    hk = _rms(h, kv_gain_ref[...]).astype(BF16)
    k_ref[...] = _dot(hk, w_k_ref[...]).astype(BF16)
    _store_token_major_t(vt_ref, _dot_nt(w_vt_ref[...], hk))
    hq = _rms(h, q_gain_ref[...]).astype(BF16)
    _store_token_major_t(qt_ref, _dot_nt(w_qt_ref[...], hq), scale)
    g2_ref[...] = _dot(hq, w_g_ref[...])


def _mid(x2, o, g, p, ple_gain, kv_gain, q_gain, w_out, w_gate, w_ple, w_k, w_vt, w_qt, w_g, scale):
    n_tok = x2.shape[0]
    tm = TM_MID
    row = lambda w: pl.BlockSpec((tm, w), lambda i: (i, 0))
    sq = _const_spec((D_MODEL, D_MODEL))
    vec = _const_spec((1, D_MODEL))
    tmaj = pl.BlockSpec((tm // T, D_MODEL, T), lambda i: (i, 0, 0))
    return pl.pallas_call(
        functools.partial(_mid_kernel, scale=scale),
        grid=(n_tok // tm,),
        in_specs=[row(D_MODEL), row(D_MODEL), row(D_MODEL), row(PLE_DIM), vec, vec, vec,
                  sq, sq, _const_spec((PLE_DIM, D_MODEL)), sq, sq, sq, sq],
        out_specs=[row(D_MODEL), row(D_MODEL), tmaj, tmaj, row(D_MODEL)],
        out_shape=[jax.ShapeDtypeStruct((n_tok, D_MODEL), F32),
                   jax.ShapeDtypeStruct((n_tok, D_MODEL), BF16),
                   jax.ShapeDtypeStruct((n_tok // T, D_MODEL, T), BF16),
                   jax.ShapeDtypeStruct((n_tok // T, D_MODEL, T), BF16),
                   jax.ShapeDtypeStruct((n_tok, D_MODEL), F32)],
        compiler_params=pltpu.CompilerParams(dimension_semantics=("arbitrary",),
                                             vmem_limit_bytes=VMEM_LIMIT),
        name="mid",
    )(x2, o, g, p, ple_gain, kv_gain, q_gain, w_out, w_gate, w_ple, w_k, w_vt, w_qt, w_g)


def _diff_attn_kernel(lam_ref, qt_ref, k_ref, vt_ref, bias_ref, subln_ref, o_ref,
                      s_s, p_s, acc_s, m_s, l_s, alpha_s, *, out_scale):
    i = pl.program_id(2)
    n = i + 1
    dk = DIFF_QK_DIM
    chains = [(hh, mm) for hh in range(DIFF_GROUP) for mm in range(2)]
    row = lax.broadcasted_iota(jnp.int32, (LANES, T), 0)
    q_maps = []
    for hh, mm in chains:
        qh = qt_ref[0, hh * LANES:(hh + 1) * LANES, :]
        keep = (row < dk) if mm == 0 else (row >= dk)
        q_maps.append(jnp.where(keep, qh, jnp.zeros_like(qh)))

    def stage_p(t):
        j = i - (t - 2)
        for c, (hh, mm) in enumerate(chains):
            vb = vt_ref[j, hh * LANES:(hh + 1) * LANES, :]
            acc_s[c] = acc_s[c] * alpha_s[c] + _dot(vb, p_s[c])

    def stage_v(slot, near):
        for c, (hh, mm) in enumerate(chains):
            s = s_s[slot, c]
            if near is not None:
                s = s + bias_ref[hh, near]
            m = m_s[c]
            m_new = jnp.maximum(m, jnp.max(s, axis=0, keepdims=True))
            p = jnp.exp(s - m_new)
            alpha = jnp.exp(m - m_new)
            p_s[c] = p.astype(BF16)
            alpha_s[c] = alpha
            m_s[c] = m_new
            l_s[c] = l_s[c] * alpha + jnp.sum(p, axis=0, keepdims=True)

    def stage_s(t, slot):
        j = jnp.maximum(i - t, 0)
        start = pl.multiple_of(j * T, T)
        for c, (hh, mm) in enumerate(chains):
            s_s[slot, c] = _dot(k_ref[pl.ds(start, T), hh * LANES:(hh + 1) * LANES], q_maps[c])

    def step(t, first, last, parity):
        now = (t & 1) if parity is None else parity
        peeled = isinstance(t, int)
        if first <= 0 <= last:
            stage_s(t, now)
        if first <= 2 <= last:
            stage_p(t)
        if first <= 1 <= last:
            if peeled and t == 1:
                stage_v(1 - now, 0)
            elif peeled and t == 2:
                pl.when(n > 1)(lambda: stage_v(1 - now, 1))
            else:
                stage_v(1 - now, None)

    for c in range(len(chains)):
        acc_s[c] = jnp.zeros((DIFF_V_DIM, T), F32)
        m_s[c] = jnp.full((1, T), NEG_INF, F32)
        l_s[c] = jnp.zeros((1, T), F32)

    _pipeline(n, 3, 3, step)

    lam = lam_ref[0, 0]
    outs = []
    for hh in range(DIFF_GROUP):
        a1 = acc_s[2 * hh] * (1.0 / l_s[2 * hh])
        a2 = acc_s[2 * hh + 1] * (1.0 / l_s[2 * hh + 1])
        o = a1 - lam * a2
        ms = jnp.mean(o * o, axis=0, keepdims=True)
        outs.append(o * lax.rsqrt(ms + EPS) * subln_ref[...] * out_scale)
    o_ref[...] = jnp.concatenate(outs, axis=0).T


def _diff_attn(lam, qt, k, vt, bias, subln, batch, seq, out_scale):
    nq = seq // T
    n_tok = batch * seq
    width = DIFF_GROUP * LANES
    n_chains = 2 * DIFF_GROUP
    row_f32 = pltpu.VMEM((n_chains, 1, T), F32)
    return pl.pallas_call(
        functools.partial(_diff_attn_kernel, out_scale=out_scale),
        grid=(batch, DIFF_HEADS // DIFF_GROUP, nq),
        in_specs=[pl.BlockSpec(memory_space=pltpu.SMEM),
                  pl.BlockSpec((1, width, T), lambda b, h, i: (b * nq + i, h, 0)),
                  pl.BlockSpec((seq, width), lambda b, h, i: (b, h)),
                  pl.BlockSpec((nq, width, T), lambda b, h, i: (b, h, 0)),
                  pl.BlockSpec((DIFF_GROUP, 2, T, T), lambda b, h, i: (h, 0, 0, 0)),
                  pl.BlockSpec((DIFF_V_DIM, 1), lambda b, h, i: (0, 0))],
        out_specs=pl.BlockSpec((T, width), lambda b, h, i: (b * nq + i, h)),
        out_shape=jax.ShapeDtypeStruct((n_tok, D_MODEL), F32),
        scratch_shapes=[pltpu.VMEM((2, n_chains, T, T), F32),
                        pltpu.VMEM((n_chains, T, T), BF16),
                        pltpu.VMEM((n_chains, DIFF_V_DIM, T), F32),
                        row_f32, row_f32, row_f32],
        compiler_params=pltpu.CompilerParams(dimension_semantics=("arbitrary",) * 3,
                                             vmem_limit_bytes=VMEM_LIMIT),
        name="diff_attn",
    )(lam, qt, k, vt, bias, subln)


def _final_kernel(h_ref, o_ref, g_ref, p_ref, ple_gain_ref, fin_gain_ref,
                  w_out_ref, w_gate_ref, w_ple_ref, out_ref):
    y = (o_ref[...] * _silu(g_ref[...])).astype(BF16)
    h = h_ref[...] + _dot(y, w_out_ref[...])
    h = _ple_update(h, p_ref[...], ple_gain_ref[...], w_gate_ref[...], w_ple_ref[...])
    out_ref[...] = _rms(h, fin_gain_ref[...])


def _final(h, o, g, p, ple_gain, fin_gain, w_out, w_gate, w_ple):
    n_tok = h.shape[0]
    tm = TM_PROJ
    row = lambda w: pl.BlockSpec((tm, w), lambda i: (i, 0))
    sq = _const_spec((D_MODEL, D_MODEL))
    vec = _const_spec((1, D_MODEL))
    return pl.pallas_call(
        _final_kernel,
        grid=(n_tok // tm,),
        in_specs=[row(D_MODEL), row(D_MODEL), row(D_MODEL), row(PLE_DIM), vec, vec,
                  sq, sq, _const_spec((PLE_DIM, D_MODEL))],
        out_specs=row(D_MODEL),
        out_shape=jax.ShapeDtypeStruct((n_tok, D_MODEL), F32),
        compiler_params=pltpu.CompilerParams(dimension_semantics=("arbitrary",),
                                             vmem_limit_bytes=VMEM_LIMIT),
        name="final",
    )(h, o, g, p, ple_gain, fin_gain, w_out, w_gate, w_ple)


def _interleave_heads(w_a, w_b, heads, dim):
    d_in = w_a.shape[0]
    stacked = jnp.stack([w_a.reshape(d_in, heads, dim), w_b.reshape(d_in, heads, dim)], axis=2)
    return stacked.reshape(d_in, heads * 2 * dim)


def kernel(x, p, attn_norm, w_in_a, w_out_a, kv_norm, w_kv, w_q_b, diff_lambda, diff_subln, w_out_b, rel_bias,
           ple_norm, w_ple_proj, w_ple_gate, final_norm):
    batch, seq, d = x.shape
    depth = p.shape[0]
    assert d == D_MODEL and depth == 2 and w_in_a.shape[0] == 1 and w_q_b.shape[0] == 1
    assert seq % T == 0 and (batch * seq) % TM_PROJ == 0
    n_tok = batch * seq
    x2 = x.reshape(n_tok, d)
    p2 = p.reshape(depth, n_tok, PLE_DIM)
    vec = lambda v: v.reshape(1, -1).astype(F32)
    qk_width = DIFF_HEADS * DIFF_QK_DIM

    wq, wk, wv, wg = jnp.split(w_in_a[0], 4, axis=-1)
    wkg_a = jnp.concatenate([wk, wg], axis=1).astype(BF16)
    wqvt_a = jnp.concatenate([wq, wv], axis=1).T.astype(BF16)
    wk1, wk2, wvv = jnp.split(w_kv, [qk_width, 2 * qk_width], axis=-1)
    wq1, wq2, wg_b = jnp.split(w_q_b[0], [qk_width, 2 * qk_width], axis=-1)
    w_k_b = _interleave_heads(wk1, wk2, DIFF_HEADS, DIFF_QK_DIM).astype(BF16)
    w_qt_b = _interleave_heads(wq1, wq2, DIFF_HEADS, DIFF_QK_DIM).T.astype(BF16)
    w_vt_b = wvv.T.astype(BF16)

    lam_init = 0.8 - 0.6 * math.exp(-0.3 * 1)
    bias, lam = _bias_lam(rel_bias.astype(F32), diff_lambda[0].astype(F32), lam_init)

    k_a, g_a, qt_a, vt_a = _inproj_a(x2, vec(attn_norm[0]), wkg_a, wqvt_a, 1.0 / math.sqrt(SB_HEAD_DIM))
    o_a = _sb_attn(qt_a, k_a, vt_a, batch, seq)

    h1, k_b, vt_b, qt_b, g_b = _mid(
        x2, o_a, g_a, p2[0], vec(ple_norm[0]), vec(kv_norm), vec(attn_norm[1]),
        w_out_a[0].astype(BF16), w_ple_gate[0].astype(BF16), w_ple_proj[0].astype(BF16),
        w_k_b, w_vt_b, w_qt_b, wg_b.astype(BF16), 1.0 / math.sqrt(DIFF_QK_DIM))

    o_b = _diff_attn(lam[0:1, 0:1], qt_b, k_b, vt_b, bias, diff_subln[0].reshape(DIFF_V_DIM, 1).astype(F32),
                     batch, seq, 1.0 - lam_init)

    out = _final(h1, o_b, g_b, p2[1], vec(ple_norm[1]), vec(final_norm),
                 w_out_b[0].astype(BF16), w_ple_gate[1].astype(BF16), w_ple_proj[1].astype(BF16))
    return out.reshape(batch, seq, d)
```

```python
import functools
import math

import jax
import jax.numpy as jnp
from jax import lax
from jax.experimental import pallas as pl
from jax.experimental.pallas import tpu as pltpu

D_MODEL = 1024
PLE_DIM = 256
SB_HEAD_DIM = 64
DIFF_QK_DIM = 64
DIFF_V_DIM = 128
DIFF_HEADS = 8
N_BUCKETS = 32
MAX_DISTANCE = 128
EPS = 1e-6
NEG_INF = -1e30

LANES = 128
T = 256
SB_GROUP = 4
SB_DEAD_RUN = 128.0
DIFF_GROUP = 2
TM_PROJ = 512
TM_MID = 256
VMEM_LIMIT = 56 * 1024 * 1024

F32 = jnp.float32
BF16 = jnp.bfloat16
NT_DIMS = (((1,), (1,)), ((), ()))


def _rms(x, gain):
    ms = jnp.mean(x * x, axis=-1, keepdims=True)
    return x * lax.rsqrt(ms + EPS) * gain


def _silu(g):
    return g * jax.nn.sigmoid(g)


def _dot(a, b):
    return jnp.dot(a, b, preferred_element_type=F32)


def _dot_nt(a, b):
    return lax.dot_general(a, b, NT_DIMS, preferred_element_type=F32)


def _store_token_major_t(ref, val_t, scale=None):
    for c in range(val_t.shape[1] // T):
        blk = val_t[:, c * T:(c + 1) * T]
        if scale is not None:
            blk = blk * scale
        ref[c] = blk.astype(BF16)


def _const_spec(shape):
    return pl.BlockSpec(shape, lambda *_: (0,) * len(shape))


def _pipeline(n, n_stages, peel, step, exhausted=None):
    last = n_stages - 1
    for t in range(peel):
        step(t, 0, min(t, last), t % 2)

    def pair(tt):
        t = peel + 2 * tt
        step(t, 0, last, peel % 2)
        step(t + 1, 0, last, 1 - peel % 2)

    steady = jnp.maximum(n - peel, 0)
    pairs = steady // 2
    if exhausted is None:
        lax.fori_loop(0, pairs, lambda tt, c: (pair(tt), c)[1], 0)
        live = None
    else:
        def body(carry):
            tt, _ = carry
            pair(tt)
            return tt + 1, exhausted()

        trips, stop = lax.while_loop(lambda c: (c[0] < pairs) & (c[1] == 0), body,
                                     (jnp.int32(0), jnp.int32(0)))
        live = stop == 0

        @pl.when(stop == 1)
        def _():
            step(peel + 2 * trips, last, last, None)

    def when_live(cond):
        return pl.when(cond if live is None else (live & cond))

    @when_live(steady % 2 == 1)
    def _():
        step(n - 1, 0, last, peel % 2)

    for d in range(last):
        @when_live(n + d >= peel)
        def _():
            step(n + d, d + 1, last, None)


def _bias_lam_kernel(table_ref, lp_ref, bias_ref, lam_ref, *, lam_init):
    h = pl.program_id(0)
    key = lax.broadcasted_iota(jnp.int32, (T, T), 0)
    qry = lax.broadcasted_iota(jnp.int32, (T, T), 1)
    max_exact = N_BUCKETS // 2
    far = table_ref[N_BUCKETS - 1, h]
    for d in range(2):
        dist = qry - key + d * T
        n = jnp.maximum(dist, 0)
        nf = jnp.maximum(n, 1).astype(F32)
        large = max_exact + (jnp.log(nf / max_exact) / math.log(MAX_DISTANCE / max_exact)
                             * (N_BUCKETS - max_exact)).astype(jnp.int32)
        large = jnp.minimum(large, N_BUCKETS - 1)
        bucket = jnp.where(n < max_exact, n, large)
        val = jnp.zeros((T, T), F32)
        for b in range(N_BUCKETS):
            val = jnp.where(bucket == b, table_ref[b, h], val)
        bias_ref[0, d] = jnp.where(dist >= 0, val - far, NEG_INF)

    lp = lp_ref[...]
    s1 = jnp.sum(lp[0:1, :] * lp[1:2, :], axis=-1, keepdims=True)
    s2 = jnp.sum(lp[2:3, :] * lp[3:4, :], axis=-1, keepdims=True)
    lam = jnp.exp(s1) - jnp.exp(s2) + lam_init
    lam_ref[...] = jnp.broadcast_to(lam, lam_ref.shape)


def _bias_lam(rel_bias, lp, lam_init):
    return pl.pallas_call(
        functools.partial(_bias_lam_kernel, lam_init=lam_init),
        grid=(DIFF_HEADS,),
        in_specs=[pl.BlockSpec(memory_space=pltpu.SMEM),
                  pl.BlockSpec((4, DIFF_QK_DIM), lambda h: (0, 0))],
        out_specs=[pl.BlockSpec((1, 2, T, T), lambda h: (h, 0, 0, 0)),
                   pl.BlockSpec((8, LANES), lambda h: (0, 0))],
        out_shape=[jax.ShapeDtypeStruct((DIFF_HEADS, 2, T, T), F32),
                   jax.ShapeDtypeStruct((8, LANES), F32)],
        name="bias_lam",
    )(rel_bias, lp)


def _inproj_a_kernel(x_ref, gain_ref, wkg_ref, wqvt_ref, k_ref, g_ref, qt_ref, vt_ref, *, scale):
    hn = _rms(x_ref[...], gain_ref[...]).astype(BF16)
    k_ref[...] = _dot(hn, wkg_ref[:, :D_MODEL]).astype(BF16)
    g_ref[...] = _dot(hn, wkg_ref[:, D_MODEL:])
    _store_token_major_t(qt_ref, _dot_nt(wqvt_ref[:D_MODEL, :], hn), scale)
    _store_token_major_t(vt_ref, _dot_nt(wqvt_ref[D_MODEL:, :], hn))


def _inproj_a(x2, gain, wkg, wqvt, scale):
    n_tok = x2.shape[0]
    tm = TM_PROJ
    return pl.pallas_call(
        functools.partial(_inproj_a_kernel, scale=scale),
        grid=(n_tok // tm,),
        in_specs=[pl.BlockSpec((tm, D_MODEL), lambda i: (i, 0)),
                  _const_spec((1, D_MODEL)),
                  _const_spec((D_MODEL, 2 * D_MODEL)),
                  _const_spec((2 * D_MODEL, D_MODEL))],
        out_specs=[pl.BlockSpec((tm, D_MODEL), lambda i: (i, 0)),
                   pl.BlockSpec((tm, D_MODEL), lambda i: (i, 0)),
                   pl.BlockSpec((tm // T, D_MODEL, T), lambda i: (i, 0, 0)),
                   pl.BlockSpec((tm // T, D_MODEL, T), lambda i: (i, 0, 0))],
        out_shape=[jax.ShapeDtypeStruct((n_tok, D_MODEL), BF16),
                   jax.ShapeDtypeStruct((n_tok, D_MODEL), F32),
                   jax.ShapeDtypeStruct((n_tok // T, D_MODEL, T), BF16),
                   jax.ShapeDtypeStruct((n_tok // T, D_MODEL, T), BF16)],
        compiler_params=pltpu.CompilerParams(dimension_semantics=("arbitrary",),
                                             vmem_limit_bytes=VMEM_LIMIT),
        name="inproj_a",
    )(x2, gain, wkg, wqvt)


def _neg_abs(x):
    bits = lax.bitcast_convert_type(x, jnp.uint32) | jnp.uint32(0x80000000)
    return lax.bitcast_convert_type(bits, F32)


def _sb_attn_kernel(qt_ref, k_ref, vt_ref, o_ref,
                    z_s, sp_s, ls_s, w_s, acc_s, run_s, scale_s, sp0_s):
    i = pl.program_id(2)
    n = i + 1
    hd = SB_HEAD_DIM
    heads = range(SB_GROUP)
    row = lax.broadcasted_iota(jnp.int32, (LANES, T), 0)
    q_heads = []
    for hh in heads:
        g, half = divmod(hh, 2)
        qg = qt_ref[0, g * LANES:(g + 1) * LANES, :]
        keep = (row < hd) if half == 0 else (row >= hd)
        q_heads.append(jnp.where(keep, qg, jnp.zeros_like(qg)))
    key = lax.broadcasted_iota(jnp.int32, (T, T), 0)
    qry = lax.broadcasted_iota(jnp.int32, (T, T), 1)
    upper = jnp.where(qry > key, 1.0, 0.0).astype(BF16)
    past = key < qry

    def stage_p(t):
        j = i - (t - 3)
        for hh in heads:
            vb = vt_ref[j, hh * hd:(hh + 1) * hd, :]
            acc_s[hh] = acc_s[hh] + _dot(vb, w_s[hh]) * scale_s[hh]

    def stage_c():
        return [_dot(upper, sp_s[hh]) for hh in heads]

    def stage_e(c, slot, diag):
        for hh in heads:
            w = jnp.exp(ls_s[slot, hh] - c[hh])
            if diag:
                w = jnp.where(past, w, 0.0)
            w_s[hh] = w.astype(BF16)
            run = run_s[hh]
            scale_s[hh] = jnp.exp(-run)
            run_s[hh] = run + c[hh][0:1, :] + sp0_s[hh]

    def stage_s(slot, diag):
        sp0 = []
        for hh in heads:
            z = z_s[slot, hh]
            soft = jnp.log(1.0 + jnp.exp(_neg_abs(z)))
            sp = jnp.maximum(z, 0.0) + soft
            ls_s[slot, hh] = z - sp
            if diag:
                sp = jnp.where(past, sp, 0.0)
            sp_s[hh] = sp.astype(BF16)
            sp0.append(sp[0:1, :])
        return sp0

    def stage_z(t, slot):
        j = jnp.maximum(i - t, 0)
        start = pl.multiple_of(j * T, T)
        for hh in heads:
            g = hh // 2
            z_s[slot, hh] = _dot(k_ref[pl.ds(start, T), g * LANES:(g + 1) * LANES], q_heads[hh])

    def step(t, first, last, parity):
        now = (t & 1) if parity is None else parity
        prev = 1 - now
        peeled = isinstance(t, int)
        if first <= 3 <= last:
            stage_p(t)
        c = stage_c() if first <= 2 <= last else None
        if first <= 0 <= last:
            stage_z(t, now)
        sp0 = stage_s(prev, diag=(peeled and t == 1)) if first <= 1 <= last else None
        if c is not None:
            stage_e(c, now, diag=(peeled and t == 2))
        if sp0 is not None:
            for hh in heads:
                sp0_s[hh] = sp0[hh]

    for hh in heads:
        acc_s[hh] = jnp.zeros((hd, T), F32)
        run_s[hh] = jnp.zeros((1, T), F32)

    def exhausted():
        return (jnp.min(run_s[...]) >= SB_DEAD_RUN).astype(jnp.int32)

    _pipeline(n, 4, 3, step, exhausted)

    o_t = jnp.concatenate([acc_s[hh] for hh in heads], axis=0)
    o_ref[...] = o_t.T


def _sb_attn(qt, k, vt, batch, seq):
    nq = seq // T
    n_tok = batch * seq
    width = SB_GROUP * SB_HEAD_DIM
    groups = D_MODEL // width
    tile_bf16 = pltpu.VMEM((SB_GROUP, T, T), BF16)
    pingpong_f32 = pltpu.VMEM((2, SB_GROUP, T, T), F32)
    row_f32 = pltpu.VMEM((SB_GROUP, 1, T), F32)
    return pl.pallas_call(
        _sb_attn_kernel,
        grid=(batch, groups, nq),
        in_specs=[pl.BlockSpec((1, width, T), lambda b, h, i: (b * nq + i, h, 0)),
                  pl.BlockSpec((seq, width), lambda b, h, i: (b, h)),
                  pl.BlockSpec((nq, width, T), lambda b, h, i: (b, h, 0))],
        out_specs=pl.BlockSpec((T, width), lambda b, h, i: (b * nq + i, h)),
        out_shape=jax.ShapeDtypeStruct((n_tok, D_MODEL), F32),
        scratch_shapes=[pingpong_f32,
                        tile_bf16,
                        pingpong_f32,
                        tile_bf16,
                        pltpu.VMEM((SB_GROUP, SB_HEAD_DIM, T), F32),
                        row_f32, row_f32, row_f32],
        compiler_params=pltpu.CompilerParams(dimension_semantics=("arbitrary",) * 3,
                                             vmem_limit_bytes=VMEM_LIMIT),
        name="sb_attn",
    )(qt, k, vt)


def _ple_update(h, p, ple_gain, w_gate, w_ple):
    gate = jax.nn.sigmoid(_dot(_rms(h, ple_gain).astype(BF16), w_gate))
    return h + gate * _dot(p.astype(BF16), w_ple)


def _mid_kernel(x_ref, o_ref, g_ref, p_ref, ple_gain_ref, kv_gain_ref, q_gain_ref,
                w_out_ref, w_gate_ref, w_ple_ref, w_k_ref, w_vt_ref, w_qt_ref, w_g_ref,
                h_ref, k_ref, vt_ref, qt_ref, g2_ref, *, scale):
    y = (o_ref[...] * _silu(g_ref[...])).astype(BF16)
    h = x_ref[...] + _dot(y, w_out_ref[...])
    h = _ple_update(h, p_ref[...], ple_gain_ref[...], w_gate_ref[...], w_ple_ref[...])
    h_ref[...] = h
    hk = _rms(h, kv_gain_ref[...]).astype(BF16)
    k_ref[...] = _dot(hk, w_k_ref[...]).astype(BF16)
    _store_token_major_t(vt_ref, _dot_nt(w_vt_ref[...], hk))
    hq = _rms(h, q_gain_ref[...]).astype(BF16)
    _store_token_major_t(qt_ref, _dot_nt(w_qt_ref[...], hq), scale)
    g2_ref[...] = _dot(hq, w_g_ref[...])


def _mid(x2, o, g, p, ple_gain, kv_gain, q_gain, w_out, w_gate, w_ple, w_k, w_vt, w_qt, w_g, scale):
    n_tok = x2.shape[0]
    tm = TM_MID
    row = lambda w: pl.BlockSpec((tm, w), lambda i: (i, 0))
    sq = _const_spec((D_MODEL, D_MODEL))
    vec = _const_spec((1, D_MODEL))
    tmaj = pl.BlockSpec((tm // T, D_MODEL, T), lambda i: (i, 0, 0))
    return pl.pallas_call(
        functools.partial(_mid_kernel, scale=scale),
        grid=(n_tok // tm,),
        in_specs=[row(D_MODEL), row(D_MODEL), row(D_MODEL), row(PLE_DIM), vec, vec, vec,
                  sq, sq, _const_spec((PLE_DIM, D_MODEL)), sq, sq, sq, sq],
        out_specs=[row(D_MODEL), row(D_MODEL), tmaj, tmaj, row(D_MODEL)],
        out_shape=[jax.ShapeDtypeStruct((n_tok, D_MODEL), F32),
                   jax.ShapeDtypeStruct((n_tok, D_MODEL), BF16),
                   jax.ShapeDtypeStruct((n_tok // T, D_MODEL, T), BF16),
                   jax.ShapeDtypeStruct((n_tok // T, D_MODEL, T), BF16),
                   jax.ShapeDtypeStruct((n_tok, D_MODEL), F32)],
        compiler_params=pltpu.CompilerParams(dimension_semantics=("arbitrary",),
                                             vmem_limit_bytes=VMEM_LIMIT),
        name="mid",
    )(x2, o, g, p, ple_gain, kv_gain, q_gain, w_out, w_gate, w_ple, w_k, w_vt, w_qt, w_g)


def _diff_attn_kernel(lam_ref, qt_ref, k_ref, vt_ref, bias_ref, subln_ref, o_ref,
                      s_s, p_s, acc_s, m_s, l_s, alpha_s, *, out_scale):
    i = pl.program_id(2)
    n = i + 1
    dk = DIFF_QK_DIM
    chains = [(hh, mm) for hh in range(DIFF_GROUP) for mm in range(2)]
    row = lax.broadcasted_iota(jnp.int32, (LANES, T), 0)
    q_maps = []
    for hh, mm in chains:
        qh = qt_ref[0, hh * LANES:(hh + 1) * LANES, :]
        keep = (row < dk) if mm == 0 else (row >= dk)
        q_maps.append(jnp.where(keep, qh, jnp.zeros_like(qh)))

    def stage_p(t):
        j = i - (t - 2)
        for c, (hh, mm) in enumerate(chains):
            vb = vt_ref[j, hh * LANES:(hh + 1) * LANES, :]
            acc_s[c] = acc_s[c] * alpha_s[c] + _dot(vb, p_s[c])

    def stage_v(slot, near):
        for c, (hh, mm) in enumerate(chains):
            s = s_s[slot, c]
            if near is not None:
                s = s + bias_ref[hh, near]
            m = m_s[c]
            m_new = jnp.maximum(m, jnp.max(s, axis=0, keepdims=True))
            p = jnp.exp(s - m_new)
            alpha = jnp.exp(m - m_new)
            p_s[c] = p.astype(BF16)
            alpha_s[c] = alpha
            m_s[c] = m_new
            l_s[c] = l_s[c] * alpha + jnp.sum(p, axis=0, keepdims=True)

    def stage_s(t, slot):
        j = jnp.maximum(i - t, 0)
        start = pl.multiple_of(j * T, T)
        for c, (hh, mm) in enumerate(chains):
            s_s[slot, c] = _dot(k_ref[pl.ds(start, T), hh * LANES:(hh + 1) * LANES], q_maps[c])

    def step(t, first, last, parity):
        now = (t & 1) if parity is None else parity
        peeled = isinstance(t, int)
        if first <= 0 <= last:
            stage_s(t, now)
        if first <= 2 <= last:
            stage_p(t)
        if first <= 1 <= last:
            if peeled and t == 1:
                stage_v(1 - now, 0)
            elif peeled and t == 2:
                pl.when(n > 1)(lambda: stage_v(1 - now, 1))
            else:
                stage_v(1 - now, None)

    for c in range(len(chains)):
        acc_s[c] = jnp.zeros((DIFF_V_DIM, T), F32)
        m_s[c] = jnp.full((1, T), NEG_INF, F32)
        l_s[c] = jnp.zeros((1, T), F32)

    _pipeline(n, 3, 3, step)

    lam = lam_ref[0, 0]
    outs = []
    for hh in range(DIFF_GROUP):
        a1 = acc_s[2 * hh] * (1.0 / l_s[2 * hh])
        a2 = acc_s[2 * hh + 1] * (1.0 / l_s[2 * hh + 1])
        o = a1 - lam * a2
        ms = jnp.mean(o * o, axis=0, keepdims=True)
        outs.append(o * lax.rsqrt(ms + EPS) * subln_ref[...] * out_scale)
    o_ref[...] = jnp.concatenate(outs, axis=0).T


def _diff_attn(lam, qt, k, vt, bias, subln, batch, seq, out_scale):
    nq = seq // T
    n_tok = batch * seq
    width = DIFF_GROUP * LANES
    n_chains = 2 * DIFF_GROUP
    row_f32 = pltpu.VMEM((n_chains, 1, T), F32)
    return pl.pallas_call(
        functools.partial(_diff_attn_kernel, out_scale=out_scale),
        grid=(batch, DIFF_HEADS // DIFF_GROUP, nq),
        in_specs=[pl.BlockSpec(memory_space=pltpu.SMEM),
                  pl.BlockSpec((1, width, T), lambda b, h, i: (b * nq + i, h, 0)),
                  pl.BlockSpec((seq, width), lambda b, h, i: (b, h)),
                  pl.BlockSpec((nq, width, T), lambda b, h, i: (b, h, 0)),
                  pl.BlockSpec((DIFF_GROUP, 2, T, T), lambda b, h, i: (h, 0, 0, 0)),
                  pl.BlockSpec((DIFF_V_DIM, 1), lambda b, h, i: (0, 0))],
        out_specs=pl.BlockSpec((T, width), lambda b, h, i: (b * nq + i, h)),
        out_shape=jax.ShapeDtypeStruct((n_tok, D_MODEL), F32),
        scratch_shapes=[pltpu.VMEM((2, n_chains, T, T), F32),
                        pltpu.VMEM((n_chains, T, T), BF16),
                        pltpu.VMEM((n_chains, DIFF_V_DIM, T), F32),
                        row_f32, row_f32, row_f32],
        compiler_params=pltpu.CompilerParams(dimension_semantics=("arbitrary",) * 3,
                                             vmem_limit_bytes=VMEM_LIMIT),
        name="diff_attn",
    )(lam, qt, k, vt, bias, subln)


def _final_kernel(h_ref, o_ref, g_ref, p_ref, ple_gain_ref, fin_gain_ref,
                  w_out_ref, w_gate_ref, w_ple_ref, out_ref):
    y = (o_ref[...] * _silu(g_ref[...])).astype(BF16)
    h = h_ref[...] + _dot(y, w_out_ref[...])
    h = _ple_update(h, p_ref[...], ple_gain_ref[...], w_gate_ref[...], w_ple_ref[...])
    out_ref[...] = _rms(h, fin_gain_ref[...])


def _final(h, o, g, p, ple_gain, fin_gain, w_out, w_gate, w_ple):
    n_tok = h.shape[0]
    tm = TM_PROJ
    row = lambda w: pl.BlockSpec((tm, w), lambda i: (i, 0))
    sq = _const_spec((D_MODEL, D_MODEL))
    vec = _const_spec((1, D_MODEL))
    return pl.pallas_call(
        _final_kernel,
        grid=(n_tok // tm,),
        in_specs=[row(D_MODEL), row(D_MODEL), row(D_MODEL), row(PLE_DIM), vec, vec,
                  sq, sq, _const_spec((PLE_DIM, D_MODEL))],
        out_specs=row(D_MODEL),
        out_shape=jax.ShapeDtypeStruct((n_tok, D_MODEL), F32),
        compiler_params=pltpu.CompilerParams(dimension_semantics=("arbitrary",),
                                             vmem_limit_bytes=VMEM_LIMIT),
        name="final",
    )(h, o, g, p, ple_gain, fin_gain, w_out, w_gate, w_ple)


def _interleave_heads(w_a, w_b, heads, dim):
    d_in = w_a.shape[0]
    stacked = jnp.stack([w_a.reshape(d_in, heads, dim), w_b.reshape(d_in, heads, dim)], axis=2)
    return stacked.reshape(d_in, heads * 2 * dim)


def kernel(x, p, attn_norm, w_in_a, w_out_a, kv_norm, w_kv, w_q_b, diff_lambda, diff_subln, w_out_b, rel_bias,
           ple_norm, w_ple_proj, w_ple_gate, final_norm):
    batch, seq, d = x.shape
    depth = p.shape[0]
    assert d == D_MODEL and depth == 2 and w_in_a.shape[0] == 1 and w_q_b.shape[0] == 1
    assert seq % T == 0 and (batch * seq) % TM_PROJ == 0
    n_tok = batch * seq
    x2 = x.reshape(n_tok, d)
    p2 = p.reshape(depth, n_tok, PLE_DIM)
    vec = lambda v: v.reshape(1, -1).astype(F32)
    qk_width = DIFF_HEADS * DIFF_QK_DIM

    wq, wk, wv, wg = jnp.split(w_in_a[0], 4, axis=-1)
    wkg_a = jnp.concatenate([wk, wg], axis=1).astype(BF16)
    wqvt_a = jnp.concatenate([wq, wv], axis=1).T.astype(BF16)
    wk1, wk2, wvv = jnp.split(w_kv, [qk_width, 2 * qk_width], axis=-1)
    wq1, wq2, wg_b = jnp.split(w_q_b[0], [qk_width, 2 * qk_width], axis=-1)
    w_k_b = _interleave_heads(wk1, wk2, DIFF_HEADS, DIFF_QK_DIM).astype(BF16)
    w_qt_b = _interleave_heads(wq1, wq2, DIFF_HEADS, DIFF_QK_DIM).T.astype(BF16)
    w_vt_b = wvv.T.astype(BF16)

    lam_init = 0.8 - 0.6 * math.exp(-0.3 * 1)
    bias, lam = _bias_lam(rel_bias.astype(F32), diff_lambda[0].astype(F32), lam_init)

    k_a, g_a, qt_a, vt_a = _inproj_a(x2, vec(attn_norm[0]), wkg_a, wqvt_a, 1.0 / math.sqrt(SB_HEAD_DIM))
    o_a = _sb_attn(qt_a, k_a, vt_a, batch, seq)

    h1, k_b, vt_b, qt_b, g_b = _mid(
        x2, o_a, g_a, p2[0], vec(ple_norm[0]), vec(kv_norm), vec(attn_norm[1]),
        w_out_a[0].astype(BF16), w_ple_gate[0].astype(BF16), w_ple_proj[0].astype(BF16),
        w_k_b, w_vt_b, w_qt_b, wg_b.astype(BF16), 1.0 / math.sqrt(DIFF_QK_DIM))

    o_b = _diff_attn(lam[0:1, 0:1], qt_b, k_b, vt_b, bias, diff_subln[0].reshape(DIFF_V_DIM, 1).astype(F32),
                     batch, seq, 1.0 - lam_init)

    out = _final(h1, o_b, g_b, p2[1], vec(ple_norm[1]), vec(final_norm),
                 w_out_b[0].astype(BF16), w_ple_gate[1].astype(BF16), w_ple_proj[1].astype(BF16))
    return out.reshape(batch, seq, d)
```

```python
import functools
import math

import jax
import jax.numpy as jnp
from jax import lax
from jax.experimental import pallas as pl
from jax.experimental.pallas import tpu as pltpu

D_MODEL = 1024
PLE_DIM = 256
SB_HEAD_DIM = 64
DIFF_QK_DIM = 64
DIFF_V_DIM = 128
DIFF_HEADS = 8
N_BUCKETS = 32
MAX_DISTANCE = 128
EPS = 1e-6
NEG_INF = -1e30

LANES = 128
T = 256
SB_GROUP = 4
SB_DEAD_RUN = 128.0
DIFF_GROUP = 2
TM_PROJ = 512
TM_MID = 512
VMEM_LIMIT = 56 * 1024 * 1024

F32 = jnp.float32
BF16 = jnp.bfloat16
NT_DIMS = (((1,), (1,)), ((), ()))


def _rms(x, gain):
    ms = jnp.mean(x * x, axis=-1, keepdims=True)
    return x * lax.rsqrt(ms + EPS) * gain


def _silu(g):
    return g * jax.nn.sigmoid(g)


def _dot(a, b):
    return jnp.dot(a, b, preferred_element_type=F32)


def _dot_nt(a, b):
    return lax.dot_general(a, b, NT_DIMS, preferred_element_type=F32)


def _store_token_major_t(ref, val_t, scale=None):
    for c in range(val_t.shape[1] // T):
        blk = val_t[:, c * T:(c + 1) * T]
        if scale is not None:
            blk = blk * scale
        ref[c] = blk.astype(BF16)


def _const_spec(shape):
    return pl.BlockSpec(shape, lambda *_: (0,) * len(shape), pipeline_mode=pl.Buffered(1))


def _pipeline(n, n_stages, peel, step, exhausted=None):
    last = n_stages - 1
    for t in range(peel):
        step(t, 0, min(t, last), t % 2)

    def pair(tt):
        t = peel + 2 * tt
        step(t, 0, last, peel % 2)
        step(t + 1, 0, last, 1 - peel % 2)

    steady = jnp.maximum(n - peel, 0)
    pairs = steady // 2
    if exhausted is None:
        lax.fori_loop(0, pairs, lambda tt, c: (pair(tt), c)[1], 0)
        live = None
    else:
        def body(carry):
            tt, _ = carry
            pair(tt)
            return tt + 1, exhausted()

        stop0 = jnp.where(n >= peel - last + 1, exhausted(), 0)
        trips, stop = lax.while_loop(lambda c: (c[0] < pairs) & (c[1] == 0), body, (jnp.int32(0), stop0))
        live = stop == 0

        @pl.when(stop == 1)
        def _():
            step(peel + 2 * trips, last, last, None)

    def when_live(cond):
        return pl.when(cond if live is None else (live & cond))

    @when_live(steady % 2 == 1)
    def _():
        step(n - 1, 0, last, peel % 2)

    for d in range(last):
        @when_live(n + d >= peel)
        def _():
            step(n + d, d + 1, last, None)


def _bias_lam_kernel(table_ref, lp_ref, bias_ref, lam_ref, *, lam_init):
    h = pl.program_id(0)
    key = lax.broadcasted_iota(jnp.int32, (T, T), 0)
    qry = lax.broadcasted_iota(jnp.int32, (T, T), 1)
    max_exact = N_BUCKETS // 2
    far = table_ref[N_BUCKETS - 1, h]
    for d in range(2):
        dist = qry - key + d * T
        n = jnp.maximum(dist, 0)
        nf = jnp.maximum(n, 1).astype(F32)
        large = max_exact + (jnp.log(nf / max_exact) / math.log(MAX_DISTANCE / max_exact)
                             * (N_BUCKETS - max_exact)).astype(jnp.int32)
        large = jnp.minimum(large, N_BUCKETS - 1)
        bucket = jnp.where(n < max_exact, n, large)
        val = jnp.zeros((T, T), F32)
        for b in range(N_BUCKETS):
            val = jnp.where(bucket == b, table_ref[b, h], val)
        bias_ref[0, d] = jnp.where(dist >= 0, val - far, NEG_INF)

    lp = lp_ref[...]
    s1 = jnp.sum(lp[0:1, :] * lp[1:2, :], axis=-1, keepdims=True)
    s2 = jnp.sum(lp[2:3, :] * lp[3:4, :], axis=-1, keepdims=True)
    lam = jnp.exp(s1) - jnp.exp(s2) + lam_init
    lam_ref[...] = jnp.broadcast_to(lam, lam_ref.shape)


def _bias_lam(rel_bias, lp, lam_init):
    return pl.pallas_call(
        functools.partial(_bias_lam_kernel, lam_init=lam_init),
        grid=(DIFF_HEADS,),
        in_specs=[pl.BlockSpec(memory_space=pltpu.SMEM),
                  pl.BlockSpec((4, DIFF_QK_DIM), lambda h: (0, 0))],
        out_specs=[pl.BlockSpec((1, 2, T, T), lambda h: (h, 0, 0, 0)),
                   pl.BlockSpec((8, LANES), lambda h: (0, 0))],
        out_shape=[jax.ShapeDtypeStruct((DIFF_HEADS, 2, T, T), F32),
                   jax.ShapeDtypeStruct((8, LANES), F32)],
        name="bias_lam",
    )(rel_bias, lp)


def _inproj_a_kernel(x_ref, gain_ref, wkg_ref, wqvt_ref, k_ref, g_ref, qt_ref, vt_ref, *, scale):
    hn = _rms(x_ref[...], gain_ref[...]).astype(BF16)
    k_ref[...] = _dot(hn, wkg_ref[:, :D_MODEL]).astype(BF16)
    g_ref[...] = _dot(hn, wkg_ref[:, D_MODEL:])
    _store_token_major_t(qt_ref, _dot_nt(wqvt_ref[:D_MODEL, :], hn), scale)
    _store_token_major_t(vt_ref, _dot_nt(wqvt_ref[D_MODEL:, :], hn))


def _inproj_a(x2, gain, wkg, wqvt, scale):
    n_tok = x2.shape[0]
    tm = TM_PROJ
    return pl.pallas_call(
        functools.partial(_inproj_a_kernel, scale=scale),
        grid=(n_tok // tm,),
        in_specs=[pl.BlockSpec((tm, D_MODEL), lambda i: (i, 0)),
                  _const_spec((1, D_MODEL)),
                  _const_spec((D_MODEL, 2 * D_MODEL)),
                  _const_spec((2 * D_MODEL, D_MODEL))],
        out_specs=[pl.BlockSpec((tm, D_MODEL), lambda i: (i, 0)),
                   pl.BlockSpec((tm, D_MODEL), lambda i: (i, 0)),
                   pl.BlockSpec((tm // T, D_MODEL, T), lambda i: (i, 0, 0)),
                   pl.BlockSpec((tm // T, D_MODEL, T), lambda i: (i, 0, 0))],
        out_shape=[jax.ShapeDtypeStruct((n_tok, D_MODEL), BF16),
                   jax.ShapeDtypeStruct((n_tok, D_MODEL), F32),
                   jax.ShapeDtypeStruct((n_tok // T, D_MODEL, T), BF16),
                   jax.ShapeDtypeStruct((n_tok // T, D_MODEL, T), BF16)],
        compiler_params=pltpu.CompilerParams(dimension_semantics=("arbitrary",),
                                             vmem_limit_bytes=VMEM_LIMIT),
        name="inproj_a",
    )(x2, gain, wkg, wqvt)


def _neg_abs(x):
    bits = lax.bitcast_convert_type(x, jnp.uint32) | jnp.uint32(0x80000000)
    return lax.bitcast_convert_type(bits, F32)


def _sb_attn_kernel(qt_ref, k_ref, vt_ref, o_ref,
                    z_s, sp_s, ls_s, w_s, acc_s, run_s, scale_s, sp0_s):
    i = pl.program_id(2)
    n = i + 1
    hd = SB_HEAD_DIM
    heads = range(SB_GROUP)
    row = lax.broadcasted_iota(jnp.int32, (LANES, T), 0)
    q_heads = []
    for hh in heads:
        g, half = divmod(hh, 2)
        qg = qt_ref[0, g * LANES:(g + 1) * LANES, :]
        keep = (row < hd) if half == 0 else (row >= hd)
        q_heads.append(jnp.where(keep, qg, jnp.zeros_like(qg)))
    key = lax.broadcasted_iota(jnp.int32, (T, T), 0)
    qry = lax.broadcasted_iota(jnp.int32, (T, T), 1)
    upper = jnp.where(qry > key, 1.0, 0.0).astype(BF16)
    past = key < qry

    def stage_p(t):
        j = i - (t - 3)
        for hh in heads:
            vb = vt_ref[j, hh * hd:(hh + 1) * hd, :]
            acc_s[hh] = acc_s[hh] + _dot(vb, w_s[hh]) * scale_s[hh]

    def stage_c():
        return [_dot(upper, sp_s[hh]) for hh in heads]

    def stage_e(c, slot, diag):
        for hh in heads:
            w = jnp.exp(ls_s[slot, hh] - c[hh])
            if diag:
                w = jnp.where(past, w, 0.0)
            w_s[hh] = w.astype(BF16)
            run = run_s[hh]
            scale_s[hh] = jnp.exp(-run)
            run_s[hh] = run + c[hh][0:1, :] + sp0_s[hh]

    def stage_s(slot, diag):
        sp0 = []
        for hh in heads:
            z = z_s[slot, hh]
            soft = jnp.log(1.0 + jnp.exp(_neg_abs(z)))
            sp = jnp.maximum(z, 0.0) + soft
            ls_s[slot, hh] = z - sp
            if diag:
                sp = jnp.where(past, sp, 0.0)
            sp_s[hh] = sp.astype(BF16)
            sp0.append(sp[0:1, :])
        return sp0

    def stage_z(t, slot):
        j = jnp.maximum(i - t, 0)
        start = pl.multiple_of(j * T, T)
        for hh in heads:
            g = hh // 2
            z_s[slot, hh] = _dot(k_ref[pl.ds(start, T), g * LANES:(g + 1) * LANES], q_heads[hh])

    def step(t, first, last, parity):
        now = (t & 1) if parity is None else parity
        prev = 1 - now
        peeled = isinstance(t, int)
        if first <= 3 <= last:
            stage_p(t)
        c = stage_c() if first <= 2 <= last else None
        if first <= 0 <= last:
            stage_z(t, now)
        sp0 = stage_s(prev, diag=(peeled and t == 1)) if first <= 1 <= last else None
        if c is not None:
            stage_e(c, now, diag=(peeled and t == 2))
        if sp0 is not None:
            for hh in heads:
                sp0_s[hh] = sp0[hh]

    for hh in heads:
        acc_s[hh] = jnp.zeros((hd, T), F32)
        run_s[hh] = jnp.zeros((1, T), F32)

    def exhausted():
        return (jnp.min(run_s[...]) >= SB_DEAD_RUN).astype(jnp.int32)

    _pipeline(n, 4, 4, step, exhausted)

    o_t = jnp.concatenate([acc_s[hh] for hh in heads], axis=0)
    o_ref[...] = o_t.T


def _sb_attn(qt, k, vt, batch, seq):
    nq = seq // T
    n_tok = batch * seq
    width = SB_GROUP * SB_HEAD_DIM
    groups = D_MODEL // width
    tile_bf16 = pltpu.VMEM((SB_GROUP, T, T), BF16)
    pingpong_f32 = pltpu.VMEM((2, SB_GROUP, T, T), F32)
    row_f32 = pltpu.VMEM((SB_GROUP, 1, T), F32)
    return pl.pallas_call(
        _sb_attn_kernel,
        grid=(batch, groups, nq),
        in_specs=[pl.BlockSpec((1, width, T), lambda b, h, i: (b * nq + i, h, 0)),
                  pl.BlockSpec((seq, width), lambda b, h, i: (b, h)),
                  pl.BlockSpec((nq, width, T), lambda b, h, i: (b, h, 0))],
        out_specs=pl.BlockSpec((T, width), lambda b, h, i: (b * nq + i, h)),
        out_shape=jax.ShapeDtypeStruct((n_tok, D_MODEL), F32),
        scratch_shapes=[pingpong_f32,
                        tile_bf16,
                        pingpong_f32,
                        tile_bf16,
                        pltpu.VMEM((SB_GROUP, SB_HEAD_DIM, T), F32),
                        row_f32, row_f32, row_f32],
        compiler_params=pltpu.CompilerParams(dimension_semantics=("arbitrary",) * 3,
                                             vmem_limit_bytes=VMEM_LIMIT),
        name="sb_attn",
    )(qt, k, vt)


def _ple_update(h, p, ple_gain, w_gate, w_ple):
    gate = jax.nn.sigmoid(_dot(_rms(h, ple_gain).astype(BF16), w_gate))
    return h + gate * _dot(p.astype(BF16), w_ple)


def _mid_kernel(x_ref, o_ref, g_ref, p_ref, ple_gain_ref, kv_gain_ref, q_gain_ref,
                w_out_ref, w_gate_ref, w_ple_ref, w_k_ref, w_vt_ref, w_qt_ref, w_g_ref,
                h_ref, k_ref, vt_ref, qt_ref, g2_ref, *, scale):
    y = (o_ref[...] * _silu(g_ref[...])).astype(BF16)
    h = x_ref[...] + _dot(y, w_out_ref[...])
    h = _ple_update(h, p_ref[...], ple_gain_ref[...], w_gate_ref[...], w_ple_ref[...])
    h_ref[...] = h
    hk = _rms(h, kv_gain_ref[...]).astype(BF16)
    k_ref[...] = _dot(hk, w_k_ref[...]).astype(BF16)
    _store_token_major_t(vt_ref, _dot_nt(w_vt_ref[...], hk))
    hq = _rms(h, q_gain_ref[...]).astype(BF16)
    _store_token_major_t(qt_ref, _dot_nt(w_qt_ref[...], hq), scale)
    g2_ref[...] = _dot(hq, w_g_ref[...])


def _mid(x2, o, g, p, ple_gain, kv_gain, q_gain, w_out, w_gate, w_ple, w_k, w_vt, w_qt, w_g, scale):
    n_tok = x2.shape[0]
    tm = TM_MID
    row = lambda w: pl.BlockSpec((tm, w), lambda i: (i, 0))
    sq = _const_spec((D_MODEL, D_MODEL))
    vec = _const_spec((1, D_MODEL))
    tmaj = pl.BlockSpec((tm // T, D_MODEL, T), lambda i: (i, 0, 0))
    return pl.pallas_call(
        functools.partial(_mid_kernel, scale=scale),
        grid=(n_tok // tm,),
        in_specs=[row(D_MODEL), row(D_MODEL), row(D_MODEL), row(PLE_DIM), vec, vec, vec,
                  sq, sq, _const_spec((PLE_DIM, D_MODEL)), sq, sq, sq, sq],
        out_specs=[row(D_MODEL), row(D_MODEL), tmaj, tmaj, row(D_MODEL)],
        out_shape=[jax.ShapeDtypeStruct((n_tok, D_MODEL), F32),
                   jax.ShapeDtypeStruct((n_tok, D_MODEL), BF16),
                   jax.ShapeDtypeStruct((n_tok // T, D_MODEL, T), BF16),
                   jax.ShapeDtypeStruct((n_tok // T, D_MODEL, T), BF16),
                   jax.ShapeDtypeStruct((n_tok, D_MODEL), F32)],
        compiler_params=pltpu.CompilerParams(dimension_semantics=("arbitrary",),
                                             vmem_limit_bytes=VMEM_LIMIT),
        name="mid",
    )(x2, o, g, p, ple_gain, kv_gain, q_gain, w_out, w_gate, w_ple, w_k, w_vt, w_qt, w_g)


def _diff_attn_kernel(lam_ref, qt_ref, k_ref, vt_ref, bias_ref, subln_ref, o_ref,
                      s_s, p_s, acc_s, m_s, l_s, alpha_s, *, out_scale):
    i = pl.program_id(2)
    n = i + 1
    dk = DIFF_QK_DIM
    chains = [(hh, mm) for hh in range(DIFF_GROUP) for mm in range(2)]
    row = lax.broadcasted_iota(jnp.int32, (LANES, T), 0)
    q_maps = []
    for hh, mm in chains:
        qh = qt_ref[0, hh * LANES:(hh + 1) * LANES, :]
        keep = (row < dk) if mm == 0 else (row >= dk)
        q_maps.append(jnp.where(keep, qh, jnp.zeros_like(qh)))

    def stage_p(t):
        j = i - (t - 2)
        for c, (hh, mm) in enumerate(chains):
            vb = vt_ref[j, hh * LANES:(hh + 1) * LANES, :]
            acc_s[c] = acc_s[c] * alpha_s[c] + _dot(vb, p_s[c])

    def stage_v(slot):
        for c in range(len(chains)):
            m = m_s[c]
            s = s_s[slot, c]
            m_new = jnp.maximum(m, jnp.max(s, axis=0, keepdims=True))
            p = jnp.exp(s - m_new)
            alpha = jnp.exp(m - m_new)
            p_s[c] = p.astype(BF16)
            alpha_s[c] = alpha
            m_s[c] = m_new
            l_s[c] = l_s[c] * alpha + jnp.sum(p, axis=0, keepdims=True)

    def stage_s(t, slot, near):
        j = jnp.maximum(i - t, 0)
        start = pl.multiple_of(j * T, T)
        for c, (hh, mm) in enumerate(chains):
            s = _dot(k_ref[pl.ds(start, T), hh * LANES:(hh + 1) * LANES], q_maps[c])
            if near is not None:
                s = s + bias_ref[hh, near]
            s_s[slot, c] = s

    def step(t, first, last, parity):
        now = (t & 1) if parity is None else parity
        peeled = isinstance(t, int)
        if first <= 0 <= last:
            stage_s(t, now, t if peeled and t < 2 else None)
        if first <= 2 <= last:
            stage_p(t)
        if first <= 1 <= last:
            if peeled and t == 2:
                pl.when(n > 1)(lambda: stage_v(1 - now))
            else:
                stage_v(1 - now)

    for c in range(len(chains)):
        acc_s[c] = jnp.zeros((DIFF_V_DIM, T), F32)
        m_s[c] = jnp.full((1, T), NEG_INF, F32)
        l_s[c] = jnp.zeros((1, T), F32)

    _pipeline(n, 3, 3, step)

    lam = lam_ref[0, 0]
    outs = []
    for hh in range(DIFF_GROUP):
        a1 = acc_s[2 * hh] * (1.0 / l_s[2 * hh])
        a2 = acc_s[2 * hh + 1] * (1.0 / l_s[2 * hh + 1])
        o = a1 - lam * a2
        ms = jnp.mean(o * o, axis=0, keepdims=True)
        outs.append(o * lax.rsqrt(ms + EPS) * subln_ref[...] * out_scale)
    o_ref[...] = jnp.concatenate(outs, axis=0).T


def _diff_attn(lam, qt, k, vt, bias, subln, batch, seq, out_scale):
    nq = seq // T
    n_tok = batch * seq
    width = DIFF_GROUP * LANES
    n_chains = 2 * DIFF_GROUP
    row_f32 = pltpu.VMEM((n_chains, 1, T), F32)
    return pl.pallas_call(
        functools.partial(_diff_attn_kernel, out_scale=out_scale),
        grid=(batch, DIFF_HEADS // DIFF_GROUP, nq),
        in_specs=[pl.BlockSpec(memory_space=pltpu.SMEM),
                  pl.BlockSpec((1, width, T), lambda b, h, i: (b * nq + i, h, 0)),
                  pl.BlockSpec((seq, width), lambda b, h, i: (b, h)),
                  pl.BlockSpec((nq, width, T), lambda b, h, i: (b, h, 0)),
                  pl.BlockSpec((DIFF_GROUP, 2, T, T), lambda b, h, i: (h, 0, 0, 0)),
                  pl.BlockSpec((DIFF_V_DIM, 1), lambda b, h, i: (0, 0))],
        out_specs=pl.BlockSpec((T, width), lambda b, h, i: (b * nq + i, h)),
        out_shape=jax.ShapeDtypeStruct((n_tok, D_MODEL), F32),
        scratch_shapes=[pltpu.VMEM((2, n_chains, T, T), F32),
                        pltpu.VMEM((n_chains, T, T), BF16),
                        pltpu.VMEM((n_chains, DIFF_V_DIM, T), F32),
                        row_f32, row_f32, row_f32],
        compiler_params=pltpu.CompilerParams(dimension_semantics=("arbitrary",) * 3,
                                             vmem_limit_bytes=VMEM_LIMIT),
        name="diff_attn",
    )(lam, qt, k, vt, bias, subln)


def _final_kernel(h_ref, o_ref, g_ref, p_ref, ple_gain_ref, fin_gain_ref,
                  w_out_ref, w_gate_ref, w_ple_ref, out_ref):
    y = (o_ref[...] * _silu(g_ref[...])).astype(BF16)
    h = h_ref[...] + _dot(y, w_out_ref[...])
    h = _ple_update(h, p_ref[...], ple_gain_ref[...], w_gate_ref[...], w_ple_ref[...])
    out_ref[...] = _rms(h, fin_gain_ref[...])


def _final(h, o, g, p, ple_gain, fin_gain, w_out, w_gate, w_ple):
    n_tok = h.shape[0]
    tm = TM_PROJ
    row = lambda w: pl.BlockSpec((tm, w), lambda i: (i, 0))
    sq = _const_spec((D_MODEL, D_MODEL))
    vec = _const_spec((1, D_MODEL))
    return pl.pallas_call(
        _final_kernel,
        grid=(n_tok // tm,),
        in_specs=[row(D_MODEL), row(D_MODEL), row(D_MODEL), row(PLE_DIM), vec, vec,
                  sq, sq, _const_spec((PLE_DIM, D_MODEL))],
        out_specs=row(D_MODEL),
        out_shape=jax.ShapeDtypeStruct((n_tok, D_MODEL), F32),
        compiler_params=pltpu.CompilerParams(dimension_semantics=("arbitrary",),
                                             vmem_limit_bytes=VMEM_LIMIT),
        name="final",
    )(h, o, g, p, ple_gain, fin_gain, w_out, w_gate, w_ple)


def _interleave_heads(w_a, w_b, heads, dim):
    d_in = w_a.shape[0]
    stacked = jnp.stack([w_a.reshape(d_in, heads, dim), w_b.reshape(d_in, heads, dim)], axis=2)
    return stacked.reshape(d_in, heads * 2 * dim)


def kernel(x, p, attn_norm, w_in_a, w_out_a, kv_norm, w_kv, w_q_b, diff_lambda, diff_subln, w_out_b, rel_bias,
           ple_norm, w_ple_proj, w_ple_gate, final_norm):
    batch, seq, d = x.shape
    depth = p.shape[0]
    assert d == D_MODEL and depth == 2 and w_in_a.shape[0] == 1 and w_q_b.shape[0] == 1
    assert seq % T == 0 and (batch * seq) % TM_PROJ == 0
    n_tok = batch * seq
    x2 = x.reshape(n_tok, d)
    p2 = p.reshape(depth, n_tok, PLE_DIM)
    vec = lambda v: v.reshape(1, -1).astype(F32)
    qk_width = DIFF_HEADS * DIFF_QK_DIM

    wq, wk, wv, wg = jnp.split(w_in_a[0], 4, axis=-1)
    wkg_a = jnp.concatenate([wk, wg], axis=1).astype(BF16)
    wqvt_a = jnp.concatenate([wq, wv], axis=1).T.astype(BF16)
    wk1, wk2, wvv = jnp.split(w_kv, [qk_width, 2 * qk_width], axis=-1)
    wq1, wq2, wg_b = jnp.split(w_q_b[0], [qk_width, 2 * qk_width], axis=-1)
    w_k_b = _interleave_heads(wk1, wk2, DIFF_HEADS, DIFF_QK_DIM).astype(BF16)
    w_qt_b = _interleave_heads(wq1, wq2, DIFF_HEADS, DIFF_QK_DIM).T.astype(BF16)
    w_vt_b = wvv.T.astype(BF16)

    lam_init = 0.8 - 0.6 * math.exp(-0.3 * 1)
    bias, lam = _bias_lam(rel_bias.astype(F32), diff_lambda[0].astype(F32), lam_init)

    k_a, g_a, qt_a, vt_a = _inproj_a(x2, vec(attn_norm[0]), wkg_a, wqvt_a, 1.0 / math.sqrt(SB_HEAD_DIM))
    o_a = _sb_attn(qt_a, k_a, vt_a, batch, seq)

    h1, k_b, vt_b, qt_b, g_b = _mid(
        x2, o_a, g_a, p2[0], vec(ple_norm[0]), vec(kv_norm), vec(attn_norm[1]),
        w_out_a[0].astype(BF16), w_ple_gate[0].astype(BF16), w_ple_proj[0].astype(BF16),
        w_k_b, w_vt_b, w_qt_b, wg_b.astype(BF16), 1.0 / math.sqrt(DIFF_QK_DIM))

    o_b = _diff_attn(lam[0:1, 0:1], qt_b, k_b, vt_b, bias, diff_subln[0].reshape(DIFF_V_DIM, 1).astype(F32),
                     batch, seq, 1.0 - lam_init)

    out = _final(h1, o_b, g_b, p2[1], vec(ple_norm[1]), vec(final_norm),
                 w_out_b[0].astype(BF16), w_ple_gate[1].astype(BF16), w_ple_proj[1].astype(BF16))
    return out.reshape(batch, seq, d)
```

```python
import functools
import math

import jax
import jax.numpy as jnp
import numpy as np
from jax import lax
from jax.experimental import pallas as pl
from jax.experimental.pallas import tpu as pltpu

D_MODEL = 1024
PLE_DIM = 256
SB_HEAD_DIM = 64
DIFF_QK_DIM = 64
DIFF_V_DIM = 128
DIFF_HEADS = 8
N_BUCKETS = 32
MAX_DISTANCE = 128
EPS = 1e-6
NEG_INF = -1e30

LANES = 128
T = 256
SB_GROUP = 4
SB_DEAD_RUN = 128.0
DIFF_GROUP = 2
TM_PROJ = 512
TM_MID = 512
V7X_VMEM_BYTES = 64 * 1024 * 1024
VMEM_LIMIT = V7X_VMEM_BYTES * 7 // 8

F32 = jnp.float32
BF16 = jnp.bfloat16
NT_DIMS = (((1,), (1,)), ((), ()))


def _rms(x, gain):
    ms = jnp.mean(x * x, axis=-1, keepdims=True)
    return x * lax.rsqrt(ms + EPS) * gain


def _silu(g):
    return g * jax.nn.sigmoid(g)


def _dot(a, b):
    return jnp.dot(a, b, preferred_element_type=F32)


def _dot_nt(a, b):
    return lax.dot_general(a, b, NT_DIMS, preferred_element_type=F32)


def _store_token_major_t(ref, val_t, scale=None):
    for c in range(val_t.shape[1] // T):
        blk = val_t[:, c * T:(c + 1) * T]
        if scale is not None:
            blk = blk * scale
        ref[c] = blk.astype(BF16)


def _const_spec(shape):
    return pl.BlockSpec(shape, lambda *_: (0,) * len(shape), pipeline_mode=pl.Buffered(1))


def _pipeline(n, n_stages, peel, step, exhausted=None):
    last = n_stages - 1
    stop0 = None
    for t in range(peel):
        if exhausted is not None and t == peel - 1 and t >= last:
            step(t, last - 1, last, t % 2)
            stop0 = jnp.where(n >= peel - last + 1, exhausted(), 0)
            pl.when(stop0 == 0)(functools.partial(step, t, 0, last - 2, t % 2))
        else:
            step(t, 0, min(t, last), t % 2)

    def pair(tt):
        t = peel + 2 * tt
        step(t, 0, last, peel % 2)
        step(t + 1, 0, last, 1 - peel % 2)

    steady = jnp.maximum(n - peel, 0)
    pairs = steady // 2
    if exhausted is None:
        lax.fori_loop(0, pairs, lambda tt, c: (pair(tt), c)[1], 0)
        live = None
    else:
        def body(carry):
            tt, _ = carry
            pair(tt)
            return tt + 1, exhausted()

        if stop0 is None:
            stop0 = jnp.int32(0)
        trips, stop = lax.while_loop(lambda c: (c[0] < pairs) & (c[1] == 0), body, (jnp.int32(0), stop0))
        live = stop == 0

        @pl.when(stop == 1)
        def _():
            step(peel + 2 * trips, last, last, None)

    def when_live(cond):
        return pl.when(cond if live is None else (live & cond))

    @when_live(steady % 2 == 1)
    def _():
        step(n - 1, 0, last, peel % 2)

    for d in range(last):
        @when_live(n + d >= peel)
        def _():
            step(n + d, d + 1, last, None)


def _bucket_tiles():
    key = np.arange(T, dtype=np.int32)[:, None]
    qry = np.arange(T, dtype=np.int32)[None, :]
    max_exact = N_BUCKETS // 2
    tiles = []
    for d in range(2):
        dist = qry - key + d * T
        n = np.maximum(dist, 0)
        nf = np.maximum(n, 1).astype(np.float32)
        large = max_exact + (np.log(nf / np.float32(max_exact)) / np.float32(math.log(MAX_DISTANCE / max_exact))
                             * np.float32(N_BUCKETS - max_exact)).astype(np.int32)
        large = np.minimum(large, N_BUCKETS - 1)
        tiles.append(np.where(dist >= 0, np.where(n < max_exact, n, large), -1).astype(np.int32))
    return np.stack(tiles)


def _bias_lam_kernel(bucket_ref, table_ref, lp_ref, bias_ref, lam_ref, *, lam_init):
    table = table_ref[0]
    far = table[:, N_BUCKETS - 1:N_BUCKETS]
    for d in range(2):
        bucket = bucket_ref[d]
        val = jnp.zeros((T, T), F32)
        for b in range(N_BUCKETS):
            val = jnp.where(bucket == b, table[:, b:b + 1], val)
        bias_ref[0, d] = jnp.where(bucket >= 0, val - far, NEG_INF)

    lp = lp_ref[...]
    s1 = jnp.sum(lp[0:1, :] * lp[1:2, :], axis=-1, keepdims=True)
    s2 = jnp.sum(lp[2:3, :] * lp[3:4, :], axis=-1, keepdims=True)
    lam = jnp.exp(s1) - jnp.exp(s2) + lam_init
    lam_ref[...] = jnp.broadcast_to(lam, lam_ref.shape)


def _bias_lam(rel_bias, lp, lam_init):
    table = rel_bias.T.reshape(DIFF_HEADS, 1, N_BUCKETS)
    return pl.pallas_call(
        functools.partial(_bias_lam_kernel, lam_init=lam_init),
        grid=(DIFF_HEADS,),
        in_specs=[pl.BlockSpec((2, T, T), lambda h: (0, 0, 0)),
                  pl.BlockSpec((1, 1, N_BUCKETS), lambda h: (h, 0, 0)),
                  pl.BlockSpec((4, DIFF_QK_DIM), lambda h: (0, 0))],
        out_specs=[pl.BlockSpec((1, 2, T, T), lambda h: (h, 0, 0, 0)),
                   pl.BlockSpec((8, LANES), lambda h: (0, 0))],
        out_shape=[jax.ShapeDtypeStruct((DIFF_HEADS, 2, T, T), F32),
                   jax.ShapeDtypeStruct((8, LANES), F32)],
        name="bias_lam",
    )(jnp.asarray(_bucket_tiles()), table, lp)


def _inproj_a_kernel(x_ref, gain_ref, wkg_ref, wqvt_ref, k_ref, g_ref, qt_ref, vt_ref, *, scale):
    hn = _rms(x_ref[...], gain_ref[...]).astype(BF16)
    k_ref[...] = _dot(hn, wkg_ref[:, :D_MODEL]).astype(BF16)
    g_ref[...] = _dot(hn, wkg_ref[:, D_MODEL:])
    _store_token_major_t(qt_ref, _dot_nt(wqvt_ref[:D_MODEL, :], hn), scale)
    _store_token_major_t(vt_ref, _dot_nt(wqvt_ref[D_MODEL:, :], hn))


def _inproj_a(x2, gain, wkg, wqvt, scale):
    n_tok = x2.shape[0]
    tm = TM_PROJ
    return pl.pallas_call(
        functools.partial(_inproj_a_kernel, scale=scale),
        grid=(n_tok // tm,),
        in_specs=[pl.BlockSpec((tm, D_MODEL), lambda i: (i, 0)),
                  _const_spec((1, D_MODEL)),
                  _const_spec((D_MODEL, 2 * D_MODEL)),
                  _const_spec((2 * D_MODEL, D_MODEL))],
        out_specs=[pl.BlockSpec((tm, D_MODEL), lambda i: (i, 0)),
                   pl.BlockSpec((tm, D_MODEL), lambda i: (i, 0)),
                   pl.BlockSpec((tm // T, D_MODEL, T), lambda i: (i, 0, 0)),
                   pl.BlockSpec((tm // T, D_MODEL, T), lambda i: (i, 0, 0))],
        out_shape=[jax.ShapeDtypeStruct((n_tok, D_MODEL), BF16),
                   jax.ShapeDtypeStruct((n_tok, D_MODEL), F32),
                   jax.ShapeDtypeStruct((n_tok // T, D_MODEL, T), BF16),
                   jax.ShapeDtypeStruct((n_tok // T, D_MODEL, T), BF16)],
        compiler_params=pltpu.CompilerParams(dimension_semantics=("arbitrary",),
                                             vmem_limit_bytes=VMEM_LIMIT),
        name="inproj_a",
    )(x2, gain, wkg, wqvt)


def _sb_attn_kernel(qt_ref, k_ref, vt_ref, o_ref,
                    z_s, sp_s, ls_s, w_s, acc_s, run_s, scale_s, sp0_s):
    i = pl.program_id(2)
    n = i + 1
    hd = SB_HEAD_DIM
    heads = range(SB_GROUP)
    row = lax.broadcasted_iota(jnp.int32, (LANES, T), 0)
    q_heads = []
    for hh in heads:
        g, half = divmod(hh, 2)
        qg = qt_ref[0, g * LANES:(g + 1) * LANES, :]
        keep = (row < hd) if half == 0 else (row >= hd)
        q_heads.append(jnp.where(keep, qg, jnp.zeros_like(qg)))
    key = lax.broadcasted_iota(jnp.int32, (T, T), 0)
    qry = lax.broadcasted_iota(jnp.int32, (T, T), 1)
    upper = jnp.where(qry > key, 1.0, 0.0).astype(BF16)
    past = key < qry

    def stage_p(t):
        j = i - (t - 3)
        for hh in heads:
            vb = vt_ref[j, hh * hd:(hh + 1) * hd, :]
            acc_s[hh] = acc_s[hh] + _dot(vb, w_s[hh]) * scale_s[hh]

    def stage_c():
        return [_dot(upper, sp_s[hh]) for hh in heads]

    def stage_e(c, slot, diag):
        for hh in heads:
            w = jnp.exp(ls_s[slot, hh] - c[hh])
            if diag:
                w = jnp.where(past, w, 0.0)
            w_s[hh] = w.astype(BF16)
            run = run_s[hh]
            scale_s[hh] = jnp.exp(-run)
            run_s[hh] = run + c[hh][0:1, :] + sp0_s[hh]

    def stage_s(slot, diag):
        sp0 = []
        for hh in heads:
            z = z_s[slot, hh]
            soft = jnp.log(1.0 + jnp.exp(-jnp.abs(z)))
            sp = jnp.maximum(z, 0.0) + soft
            ls_s[slot, hh] = z - sp
            if diag:
                sp = jnp.where(past, sp, 0.0)
            sp_s[hh] = sp.astype(BF16)
            sp0.append(sp[0:1, :])
        return sp0

    def stage_z(t, slot):
        j = jnp.maximum(i - t, 0)
        start = pl.multiple_of(j * T, T)
        for hh in heads:
            g = hh // 2
            z_s[slot, hh] = _dot(k_ref[pl.ds(start, T), g * LANES:(g + 1) * LANES], q_heads[hh])

    def step(t, first, last, parity):
        now = (t & 1) if parity is None else parity
        prev = 1 - now
        peeled = isinstance(t, int)
        if first <= 3 <= last:
            stage_p(t)
        c = stage_c() if first <= 2 <= last else None
        if first <= 0 <= last:
            stage_z(t, now)
        sp0 = stage_s(prev, diag=(peeled and t == 1)) if first <= 1 <= last else None
        if c is not None:
            stage_e(c, now, diag=(peeled and t == 2))
        if sp0 is not None:
            for hh in heads:
                sp0_s[hh] = sp0[hh]

    for hh in heads:
        acc_s[hh] = jnp.zeros((hd, T), F32)
        run_s[hh] = jnp.zeros((1, T), F32)

    def exhausted():
        return (jnp.min(run_s[...]) >= SB_DEAD_RUN).astype(jnp.int32)

    _pipeline(n, 4, 4, step, exhausted)

    o_t = jnp.concatenate([acc_s[hh] for hh in heads], axis=0)
    o_ref[...] = o_t.T


def _sb_attn(qt, k, vt, batch, seq):
    nq = seq // T
    n_tok = batch * seq
    width = SB_GROUP * SB_HEAD_DIM
    groups = D_MODEL // width
    tile_bf16 = pltpu.VMEM((SB_GROUP, T, T), BF16)
    pingpong_f32 = pltpu.VMEM((2, SB_GROUP, T, T), F32)
    row_f32 = pltpu.VMEM((SB_GROUP, 1, T), F32)
    return pl.pallas_call(
        _sb_attn_kernel,
        grid=(batch, groups, nq),
        in_specs=[pl.BlockSpec((1, width, T), lambda b, h, i: (b * nq + i, h, 0)),
                  pl.BlockSpec((seq, width), lambda b, h, i: (b, h)),
                  pl.BlockSpec((nq, width, T), lambda b, h, i: (b, h, 0))],
        out_specs=pl.BlockSpec((T, width), lambda b, h, i: (b * nq + i, h)),
        out_shape=jax.ShapeDtypeStruct((n_tok, D_MODEL), F32),
        scratch_shapes=[pingpong_f32,
                        tile_bf16,
                        pingpong_f32,
                        tile_bf16,
                        pltpu.VMEM((SB_GROUP, SB_HEAD_DIM, T), F32),
                        row_f32, row_f32, row_f32],
        compiler_params=pltpu.CompilerParams(dimension_semantics=("arbitrary",) * 3,
                                             vmem_limit_bytes=VMEM_LIMIT),
        name="sb_attn",
    )(qt, k, vt)


def _ple_update(h, p, ple_gain, w_gate, w_ple):
    gate = jax.nn.sigmoid(_dot(_rms(h, ple_gain).astype(BF16), w_gate))
    return h + gate * _dot(p.astype(BF16), w_ple)


def _mid_kernel(x_ref, o_ref, g_ref, p_ref, ple_gain_ref, kv_gain_ref, q_gain_ref,
                w_out_ref, w_gate_ref, w_ple_ref, w_k_ref, w_vt_ref, w_qt_ref, w_g_ref,
                h_ref, k_ref, vt_ref, qt_ref, g2_ref, *, scale):
    y = (o_ref[...] * _silu(g_ref[...])).astype(BF16)
    h = x_ref[...] + _dot(y, w_out_ref[...])
    h = _ple_update(h, p_ref[...], ple_gain_ref[...], w_gate_ref[...], w_ple_ref[...])
    h_ref[...] = h
    hk = _rms(h, kv_gain_ref[...]).astype(BF16)
    k_ref[...] = _dot(hk, w_k_ref[...]).astype(BF16)
    _store_token_major_t(vt_ref, _dot_nt(w_vt_ref[...], hk))
    hq = _rms(h, q_gain_ref[...]).astype(BF16)
    _store_token_major_t(qt_ref, _dot_nt(w_qt_ref[...], hq), scale)
    g2_ref[...] = _dot(hq, w_g_ref[...])


def _mid(x2, o, g, p, ple_gain, kv_gain, q_gain, w_out, w_gate, w_ple, w_k, w_vt, w_qt, w_g, scale):
    n_tok = x2.shape[0]
    tm = TM_MID
    row = lambda w: pl.BlockSpec((tm, w), lambda i: (i, 0))
    sq = _const_spec((D_MODEL, D_MODEL))
    vec = _const_spec((1, D_MODEL))
    tmaj = pl.BlockSpec((tm // T, D_MODEL, T), lambda i: (i, 0, 0))
    return pl.pallas_call(
        functools.partial(_mid_kernel, scale=scale),
        grid=(n_tok // tm,),
        in_specs=[row(D_MODEL), row(D_MODEL), row(D_MODEL), row(PLE_DIM), vec, vec, vec,
                  sq, sq, _const_spec((PLE_DIM, D_MODEL)), sq, sq, sq, sq],
        out_specs=[row(D_MODEL), row(D_MODEL), tmaj, tmaj, row(D_MODEL)],
        out_shape=[jax.ShapeDtypeStruct((n_tok, D_MODEL), F32),
                   jax.ShapeDtypeStruct((n_tok, D_MODEL), BF16),
                   jax.ShapeDtypeStruct((n_tok // T, D_MODEL, T), BF16),
                   jax.ShapeDtypeStruct((n_tok // T, D_MODEL, T), BF16),
                   jax.ShapeDtypeStruct((n_tok, D_MODEL), F32)],
        compiler_params=pltpu.CompilerParams(dimension_semantics=("arbitrary",),
                                             vmem_limit_bytes=VMEM_LIMIT),
        name="mid",
    )(x2, o, g, p, ple_gain, kv_gain, q_gain, w_out, w_gate, w_ple, w_k, w_vt, w_qt, w_g)


def _diff_attn_kernel(lam_ref, qt_ref, k_ref, vt_ref, bias_ref, subln_ref, o_ref,
                      s_s, p_s, acc_s, m_s, l_s, alpha_s, *, out_scale):
    i = pl.program_id(2)
    n = i + 1
    dk = DIFF_QK_DIM
    chains = [(hh, mm) for hh in range(DIFF_GROUP) for mm in range(2)]
    row = lax.broadcasted_iota(jnp.int32, (LANES, T), 0)
    q_maps = []
    for hh, mm in chains:
        qh = qt_ref[0, hh * LANES:(hh + 1) * LANES, :]
        keep = (row < dk) if mm == 0 else (row >= dk)
        q_maps.append(jnp.where(keep, qh, jnp.zeros_like(qh)))

    def stage_p(t):
        j = i - (t - 2)
        for c, (hh, mm) in enumerate(chains):
            vb = vt_ref[j, hh * LANES:(hh + 1) * LANES, :]
            acc_s[c] = acc_s[c] * alpha_s[c] + _dot(vb, p_s[c])

    def stage_v(slot):
        for c in range(len(chains)):
            m = m_s[c]
            s = s_s[slot, c]
            m_new = jnp.maximum(m, jnp.max(s, axis=0, keepdims=True))
            p = jnp.exp(s - m_new)
            alpha = jnp.exp(m - m_new)
            p_s[c] = p.astype(BF16)
            alpha_s[c] = alpha
            m_s[c] = m_new
            l_s[c] = l_s[c] * alpha + jnp.sum(p, axis=0, keepdims=True)

    def stage_s(t, slot, near):
        j = jnp.maximum(i - t, 0)
        start = pl.multiple_of(j * T, T)
        for c, (hh, mm) in enumerate(chains):
            s = _dot(k_ref[pl.ds(start, T), hh * LANES:(hh + 1) * LANES], q_maps[c])
            if near is not None:
                s = s + bias_ref[hh, near]
            s_s[slot, c] = s

    def step(t, first, last, parity):
        now = (t & 1) if parity is None else parity
        peeled = isinstance(t, int)
        if first <= 0 <= last:
            stage_s(t, now, t if peeled and t < 2 else None)
        if first <= 2 <= last:
            stage_p(t)
        if first <= 1 <= last:
            if peeled and t == 2:
                pl.when(n > 1)(lambda: stage_v(1 - now))
            else:
                stage_v(1 - now)

    for c in range(len(chains)):
        acc_s[c] = jnp.zeros((DIFF_V_DIM, T), F32)
        m_s[c] = jnp.full((1, T), NEG_INF, F32)
        l_s[c] = jnp.zeros((1, T), F32)

    _pipeline(n, 3, 3, step)

    lam = lam_ref[0:1, 0:1]
    outs = []
    for hh in range(DIFF_GROUP):
        a1 = acc_s[2 * hh] * (1.0 / l_s[2 * hh])
        a2 = acc_s[2 * hh + 1] * (1.0 / l_s[2 * hh + 1])
        o = a1 - lam * a2
        ms = jnp.mean(o * o, axis=0, keepdims=True)
        outs.append(o * lax.rsqrt(ms + EPS) * subln_ref[...] * out_scale)
    o_ref[...] = jnp.concatenate(outs, axis=0).T


def _diff_attn(lam, qt, k, vt, bias, subln, batch, seq, out_scale):
    nq = seq // T
    n_tok = batch * seq
    width = DIFF_GROUP * LANES
    n_chains = 2 * DIFF_GROUP
    row_f32 = pltpu.VMEM((n_chains, 1, T), F32)
    return pl.pallas_call(
        functools.partial(_diff_attn_kernel, out_scale=out_scale),
        grid=(batch, DIFF_HEADS // DIFF_GROUP, nq),
        in_specs=[pl.BlockSpec((8, LANES), lambda b, h, i: (0, 0)),
                  pl.BlockSpec((1, width, T), lambda b, h, i: (b * nq + i, h, 0)),
                  pl.BlockSpec((seq, width), lambda b, h, i: (b, h)),
                  pl.BlockSpec((nq, width, T), lambda b, h, i: (b, h, 0)),
                  pl.BlockSpec((DIFF_GROUP, 2, T, T), lambda b, h, i: (h, 0, 0, 0)),
                  pl.BlockSpec((DIFF_V_DIM, 1), lambda b, h, i: (0, 0))],
        out_specs=pl.BlockSpec((T, width), lambda b, h, i: (b * nq + i, h)),
        out_shape=jax.ShapeDtypeStruct((n_tok, D_MODEL), F32),
        scratch_shapes=[pltpu.VMEM((2, n_chains, T, T), F32),
                        pltpu.VMEM((n_chains, T, T), BF16),
                        pltpu.VMEM((n_chains, DIFF_V_DIM, T), F32),
                        row_f32, row_f32, row_f32],
        compiler_params=pltpu.CompilerParams(dimension_semantics=("arbitrary",) * 3,
                                             vmem_limit_bytes=VMEM_LIMIT),
        name="diff_attn",
    )(lam, qt, k, vt, bias, subln)


def _final_kernel(h_ref, o_ref, g_ref, p_ref, ple_gain_ref, fin_gain_ref,
                  w_out_ref, w_gate_ref, w_ple_ref, out_ref):
    y = (o_ref[...] * _silu(g_ref[...])).astype(BF16)
    h = h_ref[...] + _dot(y, w_out_ref[...])
    h = _ple_update(h, p_ref[...], ple_gain_ref[...], w_gate_ref[...], w_ple_ref[...])
    out_ref[...] = _rms(h, fin_gain_ref[...])


def _final(h, o, g, p, ple_gain, fin_gain, w_out, w_gate, w_ple):
    n_tok = h.shape[0]
    tm = TM_PROJ
    row = lambda w: pl.BlockSpec((tm, w), lambda i: (i, 0))
    sq = _const_spec((D_MODEL, D_MODEL))
    vec = _const_spec((1, D_MODEL))
    return pl.pallas_call(
        _final_kernel,
        grid=(n_tok // tm,),
        in_specs=[row(D_MODEL), row(D_MODEL), row(D_MODEL), row(PLE_DIM), vec, vec,
                  sq, sq, _const_spec((PLE_DIM, D_MODEL))],
        out_specs=row(D_MODEL),
        out_shape=jax.ShapeDtypeStruct((n_tok, D_MODEL), F32),
        compiler_params=pltpu.CompilerParams(dimension_semantics=("arbitrary",),
                                             vmem_limit_bytes=VMEM_LIMIT),
        name="final",
    )(h, o, g, p, ple_gain, fin_gain, w_out, w_gate, w_ple)


def _interleave_heads(w_a, w_b, heads, dim):
    d_in = w_a.shape[0]
    stacked = jnp.stack([w_a.reshape(d_in, heads, dim), w_b.reshape(d_in, heads, dim)], axis=2)
    return stacked.reshape(d_in, heads * 2 * dim)


def kernel(x, p, attn_norm, w_in_a, w_out_a, kv_norm, w_kv, w_q_b, diff_lambda, diff_subln, w_out_b, rel_bias,
           ple_norm, w_ple_proj, w_ple_gate, final_norm):
    batch, seq, d = x.shape
    depth = p.shape[0]
    assert d == D_MODEL and depth == 2 and w_in_a.shape[0] == 1 and w_q_b.shape[0] == 1
    assert seq % T == 0 and (batch * seq) % TM_PROJ == 0
    n_tok = batch * seq
    x2 = x.reshape(n_tok, d)
    p2 = p.reshape(depth, n_tok, PLE_DIM)
    vec = lambda v: v.reshape(1, -1).astype(F32)
    qk_width = DIFF_HEADS * DIFF_QK_DIM

    wq, wk, wv, wg = jnp.split(w_in_a[0], 4, axis=-1)
    wkg_a = jnp.concatenate([wk, wg], axis=1).astype(BF16)
    wqvt_a = jnp.concatenate([wq, wv], axis=1).T.astype(BF16)
    wk1, wk2, wvv = jnp.split(w_kv, [qk_width, 2 * qk_width], axis=-1)
    wq1, wq2, wg_b = jnp.split(w_q_b[0], [qk_width, 2 * qk_width], axis=-1)
    w_k_b = _interleave_heads(wk1, wk2, DIFF_HEADS, DIFF_QK_DIM).astype(BF16)
    w_qt_b = _interleave_heads(wq1, wq2, DIFF_HEADS, DIFF_QK_DIM).T.astype(BF16)
    w_vt_b = wvv.T.astype(BF16)

    lam_init = 0.8 - 0.6 * math.exp(-0.3 * 1)
    bias, lam = _bias_lam(rel_bias.astype(F32), diff_lambda[0].astype(F32), lam_init)

    k_a, g_a, qt_a, vt_a = _inproj_a(x2, vec(attn_norm[0]), wkg_a, wqvt_a, 1.0 / math.sqrt(SB_HEAD_DIM))
    o_a = _sb_attn(qt_a, k_a, vt_a, batch, seq)

    h1, k_b, vt_b, qt_b, g_b = _mid(
        x2, o_a, g_a, p2[0], vec(ple_norm[0]), vec(kv_norm), vec(attn_norm[1]),
        w_out_a[0].astype(BF16), w_ple_gate[0].astype(BF16), w_ple_proj[0].astype(BF16),
        w_k_b, w_vt_b, w_qt_b, wg_b.astype(BF16), 1.0 / math.sqrt(DIFF_QK_DIM))

    o_b = _diff_attn(lam, qt_b, k_b, vt_b, bias, diff_subln[0].reshape(DIFF_V_DIM, 1).astype(F32),
                     batch, seq, 1.0 - lam_init)

    out = _final(h1, o_b, g_b, p2[1], vec(ple_norm[1]), vec(final_norm),
                 w_out_b[0].astype(BF16), w_ple_gate[1].astype(BF16), w_ple_proj[1].astype(BF16))
    return out.reshape(batch, seq, d)
```

```python
import functools
import math

import jax
import jax.numpy as jnp
import numpy as np
from jax import lax
from jax.experimental import pallas as pl
from jax.experimental.pallas import tpu as pltpu

D_MODEL = 1024
PLE_DIM = 256
SB_HEAD_DIM = 64
DIFF_QK_DIM = 64
DIFF_V_DIM = 128
DIFF_HEADS = 8
N_BUCKETS = 32
MAX_DISTANCE = 128
EPS = 1e-6
NEG_INF = -1e30

LANES = 128
T = 256
SB_GROUP = 4
SB_DEAD_RUN = 128.0
DIFF_GROUP = 2
TM_PROJ = 512
TM_MID = 512
V7X_VMEM_BYTES = 64 * 1024 * 1024
VMEM_LIMIT = V7X_VMEM_BYTES * 7 // 8

F32 = jnp.float32
BF16 = jnp.bfloat16
NT_DIMS = (((1,), (1,)), ((), ()))


def _rms(x, gain):
    ms = jnp.mean(x * x, axis=-1, keepdims=True)
    return x * lax.rsqrt(ms + EPS) * gain


def _silu(g):
    return g * jax.nn.sigmoid(g)


def _dot(a, b):
    return jnp.dot(a, b, preferred_element_type=F32)


def _dot_nt(a, b):
    return lax.dot_general(a, b, NT_DIMS, preferred_element_type=F32)


def _store_token_major_t(ref, val_t, scale=None):
    for c in range(val_t.shape[1] // T):
        blk = val_t[:, c * T:(c + 1) * T]
        if scale is not None:
            blk = blk * scale
        ref[c] = blk.astype(BF16)


def _const_spec(shape):
    return pl.BlockSpec(shape, lambda *_: (0,) * len(shape), pipeline_mode=pl.Buffered(1))


def _pipeline(n, n_stages, peel, step, exhausted=None):
    last = n_stages - 1
    stop0 = None
    early = exhausted is not None and peel - 1 >= last and last >= 2
    for t in range(peel):
        if early and t == peel - 2:
            step(t, 1, min(t, last), t % 2)
        elif early and t == peel - 1:
            step(t, last - 1, last, t % 2)
            stop0 = jnp.where(n >= peel - last + 1, exhausted(), 0)

            @pl.when(stop0 == 0)
            def _():
                step(t - 1, 0, 0, (t - 1) % 2)
                step(t, 0, last - 2, t % 2)
        else:
            step(t, 0, min(t, last), t % 2)

    def pair(tt):
        t = peel + 2 * tt
        step(t, 0, last, peel % 2)
        step(t + 1, 0, last, 1 - peel % 2)

    steady = jnp.maximum(n - peel, 0)
    pairs = steady // 2
    if exhausted is None:
        lax.fori_loop(0, pairs, lambda tt, c: (pair(tt), c)[1], 0)
        live = None
    else:
        def body(carry):
            tt, _ = carry
            pair(tt)
            return tt + 1, exhausted()

        if stop0 is None:
            stop0 = jnp.int32(0)
        trips, stop = lax.while_loop(lambda c: (c[0] < pairs) & (c[1] == 0), body, (jnp.int32(0), stop0))
        live = stop == 0

        @pl.when(stop == 1)
        def _():
            step(peel + 2 * trips, last, last, None)

    def when_live(cond):
        return pl.when(cond if live is None else (live & cond))

    @when_live(steady % 2 == 1)
    def _():
        step(n - 1, 0, last, peel % 2)

    for d in range(last):
        @when_live(n + d >= peel)
        def _():
            step(n + d, d + 1, last, None)


def _bucket_tiles():
    key = np.arange(T, dtype=np.int32)[:, None]
    qry = np.arange(T, dtype=np.int32)[None, :]
    max_exact = N_BUCKETS // 2
    tiles = []
    for d in range(2):
        dist = qry - key + d * T
        n = np.maximum(dist, 0)
        nf = np.maximum(n, 1).astype(np.float32)
        large = max_exact + (np.log(nf / np.float32(max_exact)) / np.float32(math.log(MAX_DISTANCE / max_exact))
                             * np.float32(N_BUCKETS - max_exact)).astype(np.int32)
        large = np.minimum(large, N_BUCKETS - 1)
        tiles.append(np.where(dist >= 0, np.where(n < max_exact, n, large), -1).astype(np.int32))
    return np.stack(tiles)


def _bias_lam_kernel(bucket_ref, table_ref, lp_ref, bias_ref, lam_ref, *, lam_init):
    table = table_ref[0]
    far = table[:, N_BUCKETS - 1:N_BUCKETS]
    for d in range(2):
        bucket = bucket_ref[d]
        val = jnp.zeros((T, T), F32)
        for b in range(N_BUCKETS):
            val = jnp.where(bucket == b, table[:, b:b + 1], val)
        bias_ref[0, d] = jnp.where(bucket >= 0, val - far, NEG_INF)

    lp = lp_ref[...]
    s1 = jnp.sum(lp[0:1, :] * lp[1:2, :], axis=-1, keepdims=True)
    s2 = jnp.sum(lp[2:3, :] * lp[3:4, :], axis=-1, keepdims=True)
    lam = jnp.exp(s1) - jnp.exp(s2) + lam_init
    lam_ref[...] = jnp.broadcast_to(lam, lam_ref.shape)


def _bias_lam(rel_bias, lp, lam_init):
    table = rel_bias.T.reshape(DIFF_HEADS, 1, N_BUCKETS)
    return pl.pallas_call(
        functools.partial(_bias_lam_kernel, lam_init=lam_init),
        grid=(DIFF_HEADS,),
        in_specs=[pl.BlockSpec((2, T, T), lambda h: (0, 0, 0)),
                  pl.BlockSpec((1, 1, N_BUCKETS), lambda h: (h, 0, 0)),
                  pl.BlockSpec((4, DIFF_QK_DIM), lambda h: (0, 0))],
        out_specs=[pl.BlockSpec((1, 2, T, T), lambda h: (h, 0, 0, 0)),
                   pl.BlockSpec((8, LANES), lambda h: (0, 0))],
        out_shape=[jax.ShapeDtypeStruct((DIFF_HEADS, 2, T, T), F32),
                   jax.ShapeDtypeStruct((8, LANES), F32)],
        name="bias_lam",
    )(jnp.asarray(_bucket_tiles()), table, lp)


def _inproj_a_kernel(x_ref, gain_ref, wkg_ref, wqvt_ref, k_ref, g_ref, qt_ref, vt_ref, *, scale):
    hn = _rms(x_ref[...], gain_ref[...]).astype(BF16)
    k_ref[...] = _dot(hn, wkg_ref[:, :D_MODEL]).astype(BF16)
    g_ref[...] = _dot(hn, wkg_ref[:, D_MODEL:])
    _store_token_major_t(qt_ref, _dot_nt(wqvt_ref[:D_MODEL, :], hn), scale)
    _store_token_major_t(vt_ref, _dot_nt(wqvt_ref[D_MODEL:, :], hn))


def _inproj_a(x2, gain, wkg, wqvt, scale):
    n_tok = x2.shape[0]
    tm = TM_PROJ
    return pl.pallas_call(
        functools.partial(_inproj_a_kernel, scale=scale),
        grid=(n_tok // tm,),
        in_specs=[pl.BlockSpec((tm, D_MODEL), lambda i: (i, 0)),
                  _const_spec((1, D_MODEL)),
                  _const_spec((D_MODEL, 2 * D_MODEL)),
                  _const_spec((2 * D_MODEL, D_MODEL))],
        out_specs=[pl.BlockSpec((tm, D_MODEL), lambda i: (i, 0)),
                   pl.BlockSpec((tm, D_MODEL), lambda i: (i, 0)),
                   pl.BlockSpec((tm // T, D_MODEL, T), lambda i: (i, 0, 0)),
                   pl.BlockSpec((tm // T, D_MODEL, T), lambda i: (i, 0, 0))],
        out_shape=[jax.ShapeDtypeStruct((n_tok, D_MODEL), BF16),
                   jax.ShapeDtypeStruct((n_tok, D_MODEL), F32),
                   jax.ShapeDtypeStruct((n_tok // T, D_MODEL, T), BF16),
                   jax.ShapeDtypeStruct((n_tok // T, D_MODEL, T), BF16)],
        compiler_params=pltpu.CompilerParams(dimension_semantics=("arbitrary",),
                                             vmem_limit_bytes=VMEM_LIMIT),
        name="inproj_a",
    )(x2, gain, wkg, wqvt)


def _sb_attn_kernel(qt_ref, k_ref, vt_ref, o_ref,
                    z_s, sp_s, ls_s, w_s, acc_s, run_s, scale_s, sp0_s):
    i = pl.program_id(2)
    n = i + 1
    hd = SB_HEAD_DIM
    heads = range(SB_GROUP)
    row = lax.broadcasted_iota(jnp.int32, (LANES, T), 0)
    q_heads = []
    for hh in heads:
        g, half = divmod(hh, 2)
        qg = qt_ref[0, g * LANES:(g + 1) * LANES, :]
        keep = (row < hd) if half == 0 else (row >= hd)
        q_heads.append(jnp.where(keep, qg, jnp.zeros_like(qg)))
    key = lax.broadcasted_iota(jnp.int32, (T, T), 0)
    qry = lax.broadcasted_iota(jnp.int32, (T, T), 1)
    upper = jnp.where(qry > key, 1.0, 0.0).astype(BF16)
    past = key < qry

    def stage_p(t):
        j = i - (t - 3)
        for hh in heads:
            vb = vt_ref[j, hh * hd:(hh + 1) * hd, :]
            acc_s[hh] = acc_s[hh] + _dot(vb, w_s[hh]) * scale_s[hh]

    def stage_c():
        return [_dot(upper, sp_s[hh]) for hh in heads]

    def stage_e(c, slot, diag):
        for hh in heads:
            w = jnp.exp(ls_s[slot, hh] - c[hh])
            if diag:
                w = jnp.where(past, w, 0.0)
            w_s[hh] = w.astype(BF16)
            run = run_s[hh]
            scale_s[hh] = jnp.exp(-run)
            run_s[hh] = run + c[hh][0:1, :] + sp0_s[hh]

    def stage_s(slot, diag):
        sp0 = []
        for hh in heads:
            z = z_s[slot, hh]
            soft = jnp.log(1.0 + jnp.exp(-jnp.abs(z)))
            sp = jnp.maximum(z, 0.0) + soft
            ls_s[slot, hh] = z - sp
            if diag:
                sp = jnp.where(past, sp, 0.0)
            sp_s[hh] = sp.astype(BF16)
            sp0.append(sp[0:1, :])
        return sp0

    def stage_z(t, slot):
        j = jnp.maximum(i - t, 0)
        start = pl.multiple_of(j * T, T)
        for hh in heads:
            g = hh // 2
            z_s[slot, hh] = _dot(k_ref[pl.ds(start, T), g * LANES:(g + 1) * LANES], q_heads[hh])

    def step(t, first, last, parity):
        now = (t & 1) if parity is None else parity
        prev = 1 - now
        peeled = isinstance(t, int)
        if first <= 3 <= last:
            stage_p(t)
        c = stage_c() if first <= 2 <= last else None
        if first <= 0 <= last:
            stage_z(t, now)
        sp0 = stage_s(prev, diag=(peeled and t == 1)) if first <= 1 <= last else None
        if c is not None:
            stage_e(c, now, diag=(peeled and t == 2))
        if sp0 is not None:
            for hh in heads:
                sp0_s[hh] = sp0[hh]

    for hh in heads:
        acc_s[hh] = jnp.zeros((hd, T), F32)
        run_s[hh] = jnp.zeros((1, T), F32)

    def exhausted():
        return (jnp.min(run_s[...]) >= SB_DEAD_RUN).astype(jnp.int32)

    _pipeline(n, 4, 4, step, exhausted)

    o_t = jnp.concatenate([acc_s[hh] for hh in heads], axis=0)
    o_ref[...] = o_t.T


def _sb_attn(qt, k, vt, batch, seq):
    nq = seq // T
    n_tok = batch * seq
    width = SB_GROUP * SB_HEAD_DIM
    groups = D_MODEL // width
    tile_bf16 = pltpu.VMEM((SB_GROUP, T, T), BF16)
    pingpong_f32 = pltpu.VMEM((2, SB_GROUP, T, T), F32)
    row_f32 = pltpu.VMEM((SB_GROUP, 1, T), F32)
    return pl.pallas_call(
        _sb_attn_kernel,
        grid=(batch, groups, nq),
        in_specs=[pl.BlockSpec((1, width, T), lambda b, h, i: (b * nq + i, h, 0)),
                  pl.BlockSpec((seq, width), lambda b, h, i: (b, h)),
                  pl.BlockSpec((nq, width, T), lambda b, h, i: (b, h, 0))],
        out_specs=pl.BlockSpec((T, width), lambda b, h, i: (b * nq + i, h)),
        out_shape=jax.ShapeDtypeStruct((n_tok, D_MODEL), F32),
        scratch_shapes=[pingpong_f32,
                        tile_bf16,
                        pingpong_f32,
                        tile_bf16,
                        pltpu.VMEM((SB_GROUP, SB_HEAD_DIM, T), F32),
                        row_f32, row_f32, row_f32],
        compiler_params=pltpu.CompilerParams(dimension_semantics=("arbitrary",) * 3,
                                             vmem_limit_bytes=VMEM_LIMIT),
        name="sb_attn",
    )(qt, k, vt)


def _ple_update(h, p, ple_gain, w_gate, w_ple):
    gate = jax.nn.sigmoid(_dot(_rms(h, ple_gain).astype(BF16), w_gate))
    return h + gate * _dot(p.astype(BF16), w_ple)


def _mid_kernel(x_ref, o_ref, g_ref, p_ref, ple_gain_ref, kv_gain_ref, q_gain_ref,
                w_out_ref, w_gate_ref, w_ple_ref, w_k_ref, w_vt_ref, w_qt_ref, w_g_ref,
                h_ref, k_ref, vt_ref, qt_ref, g2_ref, *, scale):
    y = (o_ref[...] * _silu(g_ref[...])).astype(BF16)
    h = x_ref[...] + _dot(y, w_out_ref[...])
    h = _ple_update(h, p_ref[...], ple_gain_ref[...], w_gate_ref[...], w_ple_ref[...])
    h_ref[...] = h
    hk = _rms(h, kv_gain_ref[...]).astype(BF16)
    k_ref[...] = _dot(hk, w_k_ref[...]).astype(BF16)
    _store_token_major_t(vt_ref, _dot_nt(w_vt_ref[...], hk))
    hq = _rms(h, q_gain_ref[...]).astype(BF16)
    _store_token_major_t(qt_ref, _dot_nt(w_qt_ref[...], hq), scale)
    g2_ref[...] = _dot(hq, w_g_ref[...])


def _mid(x2, o, g, p, ple_gain, kv_gain, q_gain, w_out, w_gate, w_ple, w_k, w_vt, w_qt, w_g, scale):
    n_tok = x2.shape[0]
    tm = TM_MID
    row = lambda w: pl.BlockSpec((tm, w), lambda i: (i, 0))
    sq = _const_spec((D_MODEL, D_MODEL))
    vec = _const_spec((1, D_MODEL))
    tmaj = pl.BlockSpec((tm // T, D_MODEL, T), lambda i: (i, 0, 0))
    return pl.pallas_call(
        functools.partial(_mid_kernel, scale=scale),
        grid=(n_tok // tm,),
        in_specs=[row(D_MODEL), row(D_MODEL), row(D_MODEL), row(PLE_DIM), vec, vec, vec,
                  sq, sq, _const_spec((PLE_DIM, D_MODEL)), sq, sq, sq, sq],
        out_specs=[row(D_MODEL), row(D_MODEL), tmaj, tmaj, row(D_MODEL)],
        out_shape=[jax.ShapeDtypeStruct((n_tok, D_MODEL), F32),
                   jax.ShapeDtypeStruct((n_tok, D_MODEL), BF16),
                   jax.ShapeDtypeStruct((n_tok // T, D_MODEL, T), BF16),
                   jax.ShapeDtypeStruct((n_tok // T, D_MODEL, T), BF16),
                   jax.ShapeDtypeStruct((n_tok, D_MODEL), F32)],
        compiler_params=pltpu.CompilerParams(dimension_semantics=("arbitrary",),
                                             vmem_limit_bytes=VMEM_LIMIT),
        name="mid",
    )(x2, o, g, p, ple_gain, kv_gain, q_gain, w_out, w_gate, w_ple, w_k, w_vt, w_qt, w_g)


def _diff_attn_kernel(lam_ref, qt_ref, k_ref, vt_ref, bias_ref, subln_ref, o_ref,
                      s_s, p_s, acc_s, m_s, l_s, alpha_s, *, out_scale):
    i = pl.program_id(2)
    n = i + 1
    dk = DIFF_QK_DIM
    chains = [(hh, mm) for hh in range(DIFF_GROUP) for mm in range(2)]
    row = lax.broadcasted_iota(jnp.int32, (LANES, T), 0)
    q_maps = []
    for hh, mm in chains:
        qh = qt_ref[0, hh * LANES:(hh + 1) * LANES, :]
        keep = (row < dk) if mm == 0 else (row >= dk)
        q_maps.append(jnp.where(keep, qh, jnp.zeros_like(qh)))

    def stage_p(t):
        j = i - (t - 2)
        for c, (hh, mm) in enumerate(chains):
            vb = vt_ref[j, hh * LANES:(hh + 1) * LANES, :]
            acc_s[c] = acc_s[c] * alpha_s[c] + _dot(vb, p_s[c])

    def stage_v(slot):
        for c in range(len(chains)):
            m = m_s[c]
            s = s_s[slot, c]
            m_new = jnp.maximum(m, jnp.max(s, axis=0, keepdims=True))
            p = jnp.exp(s - m_new)
            alpha = jnp.exp(m - m_new)
            p_s[c] = p.astype(BF16)
            alpha_s[c] = alpha
            m_s[c] = m_new
            l_s[c] = l_s[c] * alpha + jnp.sum(p, axis=0, keepdims=True)

    def stage_s(t, slot, near):
        j = jnp.maximum(i - t, 0)
        start = pl.multiple_of(j * T, T)
        for c, (hh, mm) in enumerate(chains):
            s = _dot(k_ref[pl.ds(start, T), hh * LANES:(hh + 1) * LANES], q_maps[c])
            if near is not None:
                s = s + bias_ref[hh, near]
            s_s[slot, c] = s

    def step(t, first, last, parity):
        now = (t & 1) if parity is None else parity
        peeled = isinstance(t, int)
        if first <= 0 <= last:
            stage_s(t, now, t if peeled and t < 2 else None)
        if first <= 2 <= last:
            stage_p(t)
        if first <= 1 <= last:
            if peeled and t == 2:
                pl.when(n > 1)(lambda: stage_v(1 - now))
            else:
                stage_v(1 - now)

    for c in range(len(chains)):
        acc_s[c] = jnp.zeros((DIFF_V_DIM, T), F32)
        m_s[c] = jnp.full((1, T), NEG_INF, F32)
        l_s[c] = jnp.zeros((1, T), F32)

    _pipeline(n, 3, 3, step)

    lam = lam_ref[0:1, 0:1]
    outs = []
    for hh in range(DIFF_GROUP):
        a1 = acc_s[2 * hh] * (1.0 / l_s[2 * hh])
        a2 = acc_s[2 * hh + 1] * (1.0 / l_s[2 * hh + 1])
        o = a1 - lam * a2
        ms = jnp.mean(o * o, axis=0, keepdims=True)
        outs.append(o * lax.rsqrt(ms + EPS) * subln_ref[...] * out_scale)
    o_ref[...] = jnp.concatenate(outs, axis=0).T


def _diff_attn(lam, qt, k, vt, bias, subln, batch, seq, out_scale):
    nq = seq // T
    n_tok = batch * seq
    width = DIFF_GROUP * LANES
    n_chains = 2 * DIFF_GROUP
    row_f32 = pltpu.VMEM((n_chains, 1, T), F32)
    return pl.pallas_call(
        functools.partial(_diff_attn_kernel, out_scale=out_scale),
        grid=(batch, DIFF_HEADS // DIFF_GROUP, nq),
        in_specs=[pl.BlockSpec((8, LANES), lambda b, h, i: (0, 0)),
                  pl.BlockSpec((1, width, T), lambda b, h, i: (b * nq + i, h, 0)),
                  pl.BlockSpec((seq, width), lambda b, h, i: (b, h)),
                  pl.BlockSpec((nq, width, T), lambda b, h, i: (b, h, 0)),
                  pl.BlockSpec((DIFF_GROUP, 2, T, T), lambda b, h, i: (h, 0, 0, 0)),
                  pl.BlockSpec((DIFF_V_DIM, 1), lambda b, h, i: (0, 0))],
        out_specs=pl.BlockSpec((T, width), lambda b, h, i: (b * nq + i, h)),
        out_shape=jax.ShapeDtypeStruct((n_tok, D_MODEL), F32),
        scratch_shapes=[pltpu.VMEM((2, n_chains, T, T), F32),
                        pltpu.VMEM((n_chains, T, T), BF16),
                        pltpu.VMEM((n_chains, DIFF_V_DIM, T), F32),
                        row_f32, row_f32, row_f32],
        compiler_params=pltpu.CompilerParams(dimension_semantics=("arbitrary",) * 3,
                                             vmem_limit_bytes=VMEM_LIMIT),
        name="diff_attn",
    )(lam, qt, k, vt, bias, subln)


def _final_kernel(h_ref, o_ref, g_ref, p_ref, ple_gain_ref, fin_gain_ref,
                  w_out_ref, w_gate_ref, w_ple_ref, out_ref):
    y = (o_ref[...] * _silu(g_ref[...])).astype(BF16)
    h = h_ref[...] + _dot(y, w_out_ref[...])
    h = _ple_update(h, p_ref[...], ple_gain_ref[...], w_gate_ref[...], w_ple_ref[...])
    out_ref[...] = _rms(h, fin_gain_ref[...])


def _final(h, o, g, p, ple_gain, fin_gain, w_out, w_gate, w_ple):
    n_tok = h.shape[0]
    tm = TM_PROJ
    row = lambda w: pl.BlockSpec((tm, w), lambda i: (i, 0))
    sq = _const_spec((D_MODEL, D_MODEL))
    vec = _const_spec((1, D_MODEL))
    return pl.pallas_call(
        _final_kernel,
        grid=(n_tok // tm,),
        in_specs=[row(D_MODEL), row(D_MODEL), row(D_MODEL), row(PLE_DIM), vec, vec,
                  sq, sq, _const_spec((PLE_DIM, D_MODEL))],
        out_specs=row(D_MODEL),
        out_shape=jax.ShapeDtypeStruct((n_tok, D_MODEL), F32),
        compiler_params=pltpu.CompilerParams(dimension_semantics=("arbitrary",),
                                             vmem_limit_bytes=VMEM_LIMIT),
        name="final",
    )(h, o, g, p, ple_gain, fin_gain, w_out, w_gate, w_ple)


def _interleave_heads(w_a, w_b, heads, dim):
    d_in = w_a.shape[0]
    stacked = jnp.stack([w_a.reshape(d_in, heads, dim), w_b.reshape(d_in, heads, dim)], axis=2)
    return stacked.reshape(d_in, heads * 2 * dim)


def kernel(x, p, attn_norm, w_in_a, w_out_a, kv_norm, w_kv, w_q_b, diff_lambda, diff_subln, w_out_b, rel_bias,
           ple_norm, w_ple_proj, w_ple_gate, final_norm):
    batch, seq, d = x.shape
    depth = p.shape[0]
    assert d == D_MODEL and depth == 2 and w_in_a.shape[0] == 1 and w_q_b.shape[0] == 1
    assert seq % T == 0 and (batch * seq) % TM_PROJ == 0
    n_tok = batch * seq
    x2 = x.reshape(n_tok, d)
    p2 = p.reshape(depth, n_tok, PLE_DIM)
    vec = lambda v: v.reshape(1, -1).astype(F32)
    qk_width = DIFF_HEADS * DIFF_QK_DIM

    wq, wk, wv, wg = jnp.split(w_in_a[0], 4, axis=-1)
    wkg_a = jnp.concatenate([wk, wg], axis=1).astype(BF16)
    wqvt_a = jnp.concatenate([wq, wv], axis=1).T.astype(BF16)
    wk1, wk2, wvv = jnp.split(w_kv, [qk_width, 2 * qk_width], axis=-1)
    wq1, wq2, wg_b = jnp.split(w_q_b[0], [qk_width, 2 * qk_width], axis=-1)
    w_k_b = _interleave_heads(wk1, wk2, DIFF_HEADS, DIFF_QK_DIM).astype(BF16)
    w_qt_b = _interleave_heads(wq1, wq2, DIFF_HEADS, DIFF_QK_DIM).T.astype(BF16)
    w_vt_b = wvv.T.astype(BF16)

    lam_init = 0.8 - 0.6 * math.exp(-0.3 * 1)
    bias, lam = _bias_lam(rel_bias.astype(F32), diff_lambda[0].astype(F32), lam_init)

    k_a, g_a, qt_a, vt_a = _inproj_a(x2, vec(attn_norm[0]), wkg_a, wqvt_a, 1.0 / math.sqrt(SB_HEAD_DIM))
    o_a = _sb_attn(qt_a, k_a, vt_a, batch, seq)

    h1, k_b, vt_b, qt_b, g_b = _mid(
        x2, o_a, g_a, p2[0], vec(ple_norm[0]), vec(kv_norm), vec(attn_norm[1]),
        w_out_a[0].astype(BF16), w_ple_gate[0].astype(BF16), w_ple_proj[0].astype(BF16),
        w_k_b, w_vt_b, w_qt_b, wg_b.astype(BF16), 1.0 / math.sqrt(DIFF_QK_DIM))

    o_b = _diff_attn(lam, qt_b, k_b, vt_b, bias, diff_subln[0].reshape(DIFF_V_DIM, 1).astype(F32),
                     batch, seq, 1.0 - lam_init)

    out = _final(h1, o_b, g_b, p2[1], vec(ple_norm[1]), vec(final_norm),
                 w_out_b[0].astype(BF16), w_ple_gate[1].astype(BF16), w_ple_proj[1].astype(BF16))
    return out.reshape(batch, seq, d)
```

```python
import functools
import math

import jax
import jax.numpy as jnp
import numpy as np
from jax import lax
from jax.experimental import pallas as pl
from jax.experimental.pallas import tpu as pltpu

D_MODEL = 1024
PLE_DIM = 256
SB_HEAD_DIM = 64
DIFF_QK_DIM = 64
DIFF_V_DIM = 128
DIFF_HEADS = 8
N_BUCKETS = 32
MAX_DISTANCE = 128
EPS = 1e-6
NEG_INF = -1e30

LANES = 128
T = 256
SB_GROUP = 4
SB_DEAD_RUN = 128.0
DIFF_GROUP = 2
TM_PROJ = 512
TM_MID = 512
V7X_VMEM_BYTES = 64 * 1024 * 1024
VMEM_LIMIT = V7X_VMEM_BYTES * 7 // 8

F32 = jnp.float32
BF16 = jnp.bfloat16
NT_DIMS = (((1,), (1,)), ((), ()))


def _rms(x, gain):
    ms = jnp.mean(x * x, axis=-1, keepdims=True)
    return x * lax.rsqrt(ms + EPS) * gain


def _silu(g):
    return g * jax.nn.sigmoid(g)


def _dot(a, b):
    return jnp.dot(a, b, preferred_element_type=F32)


def _dot_nt(a, b):
    return lax.dot_general(a, b, NT_DIMS, preferred_element_type=F32)


def _store_token_major_t(ref, val_t, scale=None):
    for c in range(val_t.shape[1] // T):
        blk = val_t[:, c * T:(c + 1) * T]
        if scale is not None:
            blk = blk * scale
        ref[c] = blk.astype(BF16)


def _const_spec(shape):
    return pl.BlockSpec(shape, lambda *_: (0,) * len(shape), pipeline_mode=pl.Buffered(1))


def _pipeline(n, n_stages, peel, step, exhausted=None):
    last = n_stages - 1
    stop0 = None
    for t in range(peel):
        if exhausted is not None and t == peel - 1 and t >= last:
            step(t, last - 1, last, t % 2)
            stop0 = jnp.where(n >= peel - last + 1, exhausted(), 0)
            pl.when(stop0 == 0)(functools.partial(step, t, 0, last - 2, t % 2))
        else:
            step(t, 0, min(t, last), t % 2)

    def pair(tt):
        t = peel + 2 * tt
        step(t, 0, last, peel % 2)
        step(t + 1, 0, last, 1 - peel % 2)

    steady = jnp.maximum(n - peel, 0)
    pairs = steady // 2
    if exhausted is None:
        lax.fori_loop(0, pairs, lambda tt, c: (pair(tt), c)[1], 0)
        live = None
    else:
        def body(carry):
            tt, _ = carry
            pair(tt)
            return tt + 1, exhausted()

        if stop0 is None:
            stop0 = jnp.int32(0)
        trips, stop = lax.while_loop(lambda c: (c[0] < pairs) & (c[1] == 0), body, (jnp.int32(0), stop0))
        live = stop == 0

        @pl.when(stop == 1)
        def _():
            step(peel + 2 * trips, last, last, None)

    def when_live(cond):
        return pl.when(cond if live is None else (live & cond))

    @when_live(steady % 2 == 1)
    def _():
        step(n - 1, 0, last, peel % 2)

    for d in range(last):
        @when_live(n + d >= peel)
        def _():
            step(n + d, d + 1, last, None)


def _bucket_tiles():
    key = np.arange(T, dtype=np.int32)[:, None]
    qry = np.arange(T, dtype=np.int32)[None, :]
    max_exact = N_BUCKETS // 2
    tiles = []
    for d in range(2):
        dist = qry - key + d * T
        n = np.maximum(dist, 0)
        nf = np.maximum(n, 1).astype(np.float32)
        large = max_exact + (np.log(nf / np.float32(max_exact)) / np.float32(math.log(MAX_DISTANCE / max_exact))
                             * np.float32(N_BUCKETS - max_exact)).astype(np.int32)
        large = np.minimum(large, N_BUCKETS - 1)
        tiles.append(np.where(dist >= 0, np.where(n < max_exact, n, large), -1).astype(np.int32))
    return np.stack(tiles)


def _bias_lam_kernel(bucket_ref, table_ref, lp_ref, bias_ref, lam_ref, *, lam_init):
    table = table_ref[0]
    far = table[:, N_BUCKETS - 1:N_BUCKETS]
    for d in range(2):
        bucket = bucket_ref[d]
        val = jnp.zeros((T, T), F32)
        for b in range(N_BUCKETS):
            val = jnp.where(bucket == b, table[:, b:b + 1], val)
        bias_ref[0, d] = jnp.where(bucket >= 0, val - far, NEG_INF)

    lp = lp_ref[...]
    s1 = jnp.sum(lp[0:1, :] * lp[1:2, :], axis=-1, keepdims=True)
    s2 = jnp.sum(lp[2:3, :] * lp[3:4, :], axis=-1, keepdims=True)
    lam = jnp.exp(s1) - jnp.exp(s2) + lam_init
    lam_ref[...] = jnp.broadcast_to(lam, lam_ref.shape)


def _bias_lam(rel_bias, lp, lam_init):
    table = rel_bias.T.reshape(DIFF_HEADS, 1, N_BUCKETS)
    return pl.pallas_call(
        functools.partial(_bias_lam_kernel, lam_init=lam_init),
        grid=(DIFF_HEADS,),
        in_specs=[pl.BlockSpec((2, T, T), lambda h: (0, 0, 0)),
                  pl.BlockSpec((1, 1, N_BUCKETS), lambda h: (h, 0, 0)),
                  pl.BlockSpec((4, DIFF_QK_DIM), lambda h: (0, 0))],
        out_specs=[pl.BlockSpec((1, 2, T, T), lambda h: (h, 0, 0, 0)),
                   pl.BlockSpec((8, LANES), lambda h: (0, 0))],
        out_shape=[jax.ShapeDtypeStruct((DIFF_HEADS, 2, T, T), F32),
                   jax.ShapeDtypeStruct((8, LANES), F32)],
        name="bias_lam",
    )(jnp.asarray(_bucket_tiles()), table, lp)


def _inproj_a_kernel(x_ref, gain_ref, wk_ref, wg_ref, wqt_ref, wvt_ref, k_ref, g_ref, qt_ref, vt_ref, *, scale):
    hn = _rms(x_ref[...], gain_ref[...]).astype(BF16)
    k_ref[...] = _dot(hn, wk_ref[...]).astype(BF16)
    g_ref[...] = _dot(hn, wg_ref[...])
    _store_token_major_t(qt_ref, _dot_nt(wqt_ref[...], hn), scale)
    _store_token_major_t(vt_ref, _dot_nt(wvt_ref[...], hn))


def _inproj_a(x2, gain, wk, wg, wqt, wvt, scale):
    n_tok = x2.shape[0]
    tm = TM_PROJ
    return pl.pallas_call(
        functools.partial(_inproj_a_kernel, scale=scale),
        grid=(n_tok // tm,),
        in_specs=[pl.BlockSpec((tm, D_MODEL), lambda i: (i, 0)),
                  _const_spec((1, D_MODEL))] + [_const_spec((D_MODEL, D_MODEL))] * 4,
        out_specs=[pl.BlockSpec((tm, D_MODEL), lambda i: (i, 0)),
                   pl.BlockSpec((tm, D_MODEL), lambda i: (i, 0)),
                   pl.BlockSpec((tm // T, D_MODEL, T), lambda i: (i, 0, 0)),
                   pl.BlockSpec((tm // T, D_MODEL, T), lambda i: (i, 0, 0))],
        out_shape=[jax.ShapeDtypeStruct((n_tok, D_MODEL), BF16),
                   jax.ShapeDtypeStruct((n_tok, D_MODEL), F32),
                   jax.ShapeDtypeStruct((n_tok // T, D_MODEL, T), BF16),
                   jax.ShapeDtypeStruct((n_tok // T, D_MODEL, T), BF16)],
        compiler_params=pltpu.CompilerParams(dimension_semantics=("arbitrary",),
                                             vmem_limit_bytes=VMEM_LIMIT),
        name="inproj_a",
    )(x2, gain, wk, wg, wqt, wvt)


def _sb_attn_kernel(qt_ref, k_ref, vt_ref, o_ref,
                    z_s, sp_s, ls_s, w_s, acc_s, run_s, scale_s, sp0_s):
    i = pl.program_id(2)
    n = i + 1
    hd = SB_HEAD_DIM
    heads = range(SB_GROUP)
    row = lax.broadcasted_iota(jnp.int32, (LANES, T), 0)
    q_heads = []
    for hh in heads:
        g, half = divmod(hh, 2)
        qg = qt_ref[0, g * LANES:(g + 1) * LANES, :]
        keep = (row < hd) if half == 0 else (row >= hd)
        q_heads.append(jnp.where(keep, qg, jnp.zeros_like(qg)))
    key = lax.broadcasted_iota(jnp.int32, (T, T), 0)
    qry = lax.broadcasted_iota(jnp.int32, (T, T), 1)
    upper = jnp.where(qry > key, 1.0, 0.0).astype(BF16)
    past = key < qry

    def stage_p(t):
        j = i - (t - 3)
        for hh in heads:
            vb = vt_ref[j, hh * hd:(hh + 1) * hd, :]
            acc_s[hh] = acc_s[hh] + _dot(vb, w_s[hh]) * scale_s[hh]

    def stage_c():
        return [_dot(upper, sp_s[hh]) for hh in heads]

    def stage_e(c, slot, diag):
        for hh in heads:
            w = jnp.exp(ls_s[slot, hh] - c[hh])
            if diag:
                w = jnp.where(past, w, 0.0)
            w_s[hh] = w.astype(BF16)
            run = run_s[hh]
            scale_s[hh] = jnp.exp(-run)
            run_s[hh] = run + c[hh][0:1, :] + sp0_s[hh]

    def stage_s(slot, diag):
        sp0 = []
        for hh in heads:
            z = z_s[slot, hh]
            soft = jnp.log(1.0 + jnp.exp(-jnp.abs(z)))
            sp = jnp.maximum(z, 0.0) + soft
            ls_s[slot, hh] = z - sp
            if diag:
                sp = jnp.where(past, sp, 0.0)
            sp_s[hh] = sp.astype(BF16)
            sp0.append(sp[0:1, :])
        return sp0

    def stage_z(t, slot):
        j = jnp.maximum(i - t, 0)
        start = pl.multiple_of(j * T, T)
        for hh in heads:
            g = hh // 2
            z_s[slot, hh] = _dot(k_ref[pl.ds(start, T), g * LANES:(g + 1) * LANES], q_heads[hh])

    def step(t, first, last, parity):
        now = (t & 1) if parity is None else parity
        prev = 1 - now
        peeled = isinstance(t, int)
        if first <= 3 <= last:
            stage_p(t)
        c = stage_c() if first <= 2 <= last else None
        if first <= 0 <= last:
            stage_z(t, now)
        sp0 = stage_s(prev, diag=(peeled and t == 1)) if first <= 1 <= last else None
        if c is not None:
            stage_e(c, now, diag=(peeled and t == 2))
        if sp0 is not None:
            for hh in heads:
                sp0_s[hh] = sp0[hh]

    for hh in heads:
        acc_s[hh] = jnp.zeros((hd, T), F32)
        run_s[hh] = jnp.zeros((1, T), F32)

    def exhausted():
        return (jnp.min(run_s[...]) >= SB_DEAD_RUN).astype(jnp.int32)

    _pipeline(n, 4, 4, step, exhausted)

    o_t = jnp.concatenate([acc_s[hh] for hh in heads], axis=0)
    o_ref[...] = o_t.T


def _sb_attn(qt, k, vt, batch, seq):
    nq = seq // T
    n_tok = batch * seq
    width = SB_GROUP * SB_HEAD_DIM
    groups = D_MODEL // width
    tile_bf16 = pltpu.VMEM((SB_GROUP, T, T), BF16)
    pingpong_f32 = pltpu.VMEM((2, SB_GROUP, T, T), F32)
    row_f32 = pltpu.VMEM((SB_GROUP, 1, T), F32)
    return pl.pallas_call(
        _sb_attn_kernel,
        grid=(batch, groups, nq),
        in_specs=[pl.BlockSpec((1, width, T), lambda b, h, i: (b * nq + i, h, 0)),
                  pl.BlockSpec((seq, width), lambda b, h, i: (b, h)),
                  pl.BlockSpec((nq, width, T), lambda b, h, i: (b, h, 0))],
        out_specs=pl.BlockSpec((T, width), lambda b, h, i: (b * nq + i, h)),
        out_shape=jax.ShapeDtypeStruct((n_tok, D_MODEL), F32),
        scratch_shapes=[pingpong_f32,
                        tile_bf16,
                        pingpong_f32,
                        tile_bf16,
                        pltpu.VMEM((SB_GROUP, SB_HEAD_DIM, T), F32),
                        row_f32, row_f32, row_f32],
        compiler_params=pltpu.CompilerParams(dimension_semantics=("arbitrary",) * 3,
                                             vmem_limit_bytes=VMEM_LIMIT),
        name="sb_attn",
    )(qt, k, vt)


def _ple_update(h, p, ple_gain, w_gate, w_ple):
    gate = jax.nn.sigmoid(_dot(_rms(h, ple_gain).astype(BF16), w_gate))
    return h + gate * _dot(p.astype(BF16), w_ple)


def _mid_kernel(x_ref, o_ref, g_ref, p_ref, ple_gain_ref, kv_gain_ref, q_gain_ref,
                w_out_ref, w_gate_ref, w_ple_ref, w_k_ref, w_vt_ref, w_qt_ref, w_g_ref,
                h_ref, k_ref, vt_ref, qt_ref, g2_ref, *, scale):
    y = (o_ref[...] * _silu(g_ref[...])).astype(BF16)
    h = x_ref[...] + _dot(y, w_out_ref[...])
    h = _ple_update(h, p_ref[...], ple_gain_ref[...], w_gate_ref[...], w_ple_ref[...])
    h_ref[...] = h
    hk = _rms(h, kv_gain_ref[...]).astype(BF16)
    k_ref[...] = _dot(hk, w_k_ref[...]).astype(BF16)
    _store_token_major_t(vt_ref, _dot_nt(w_vt_ref[...], hk))
    hq = _rms(h, q_gain_ref[...]).astype(BF16)
    _store_token_major_t(qt_ref, _dot_nt(w_qt_ref[...], hq), scale)
    g2_ref[...] = _dot(hq, w_g_ref[...])


def _mid(x2, o, g, p, ple_gain, kv_gain, q_gain, w_out, w_gate, w_ple, w_k, w_vt, w_qt, w_g, scale):
    n_tok = x2.shape[0]
    tm = TM_MID
    row = lambda w: pl.BlockSpec((tm, w), lambda i: (i, 0))
    sq = _const_spec((D_MODEL, D_MODEL))
    vec = _const_spec((1, D_MODEL))
    tmaj = pl.BlockSpec((tm // T, D_MODEL, T), lambda i: (i, 0, 0))
    return pl.pallas_call(
        functools.partial(_mid_kernel, scale=scale),
        grid=(n_tok // tm,),
        in_specs=[row(D_MODEL), row(D_MODEL), row(D_MODEL), row(PLE_DIM), vec, vec, vec,
                  sq, sq, _const_spec((PLE_DIM, D_MODEL)), sq, sq, sq, sq],
        out_specs=[row(D_MODEL), row(D_MODEL), tmaj, tmaj, row(D_MODEL)],
        out_shape=[jax.ShapeDtypeStruct((n_tok, D_MODEL), F32),
                   jax.ShapeDtypeStruct((n_tok, D_MODEL), BF16),
                   jax.ShapeDtypeStruct((n_tok // T, D_MODEL, T), BF16),
                   jax.ShapeDtypeStruct((n_tok // T, D_MODEL, T), BF16),
                   jax.ShapeDtypeStruct((n_tok, D_MODEL), F32)],
        compiler_params=pltpu.CompilerParams(dimension_semantics=("arbitrary",),
                                             vmem_limit_bytes=VMEM_LIMIT),
        name="mid",
    )(x2, o, g, p, ple_gain, kv_gain, q_gain, w_out, w_gate, w_ple, w_k, w_vt, w_qt, w_g)


def _diff_attn_kernel(lam_ref, qt_ref, k_ref, vt_ref, bias_ref, subln_ref, o_ref,
                      s_s, p_s, acc_s, m_s, l_s, alpha_s, *, out_scale):
    i = pl.program_id(2)
    n = i + 1
    dk = DIFF_QK_DIM
    chains = [(hh, mm) for hh in range(DIFF_GROUP) for mm in range(2)]
    row = lax.broadcasted_iota(jnp.int32, (LANES, T), 0)
    q_maps = []
    for hh, mm in chains:
        qh = qt_ref[0, hh * LANES:(hh + 1) * LANES, :]
        keep = (row < dk) if mm == 0 else (row >= dk)
        q_maps.append(jnp.where(keep, qh, jnp.zeros_like(qh)))

    def stage_p(t):
        j = i - (t - 2)
        for c, (hh, mm) in enumerate(chains):
            vb = vt_ref[j, hh * LANES:(hh + 1) * LANES, :]
            acc_s[c] = acc_s[c] * alpha_s[c] + _dot(vb, p_s[c])

    def stage_v(slot):
        for c in range(len(chains)):
            m = m_s[c]
            s = s_s[slot, c]
            m_new = jnp.maximum(m, jnp.max(s, axis=0, keepdims=True))
            p = jnp.exp(s - m_new)
            alpha = jnp.exp(m - m_new)
            p_s[c] = p.astype(BF16)
            alpha_s[c] = alpha
            m_s[c] = m_new
            l_s[c] = l_s[c] * alpha + jnp.sum(p, axis=0, keepdims=True)

    def stage_s(t, slot, near):
        j = jnp.maximum(i - t, 0)
        start = pl.multiple_of(j * T, T)
        for c, (hh, mm) in enumerate(chains):
            s = _dot(k_ref[pl.ds(start, T), hh * LANES:(hh + 1) * LANES], q_maps[c])
            if near is not None:
                s = s + bias_ref[hh, near]
            s_s[slot, c] = s

    def step(t, first, last, parity):
        now = (t & 1) if parity is None else parity
        peeled = isinstance(t, int)
        if first <= 0 <= last:
            stage_s(t, now, t if peeled and t < 2 else None)
        if first <= 2 <= last:
            stage_p(t)
        if first <= 1 <= last:
            if peeled and t == 2:
                pl.when(n > 1)(lambda: stage_v(1 - now))
            else:
                stage_v(1 - now)

    for c in range(len(chains)):
        acc_s[c] = jnp.zeros((DIFF_V_DIM, T), F32)
        m_s[c] = jnp.full((1, T), NEG_INF, F32)
        l_s[c] = jnp.zeros((1, T), F32)

    _pipeline(n, 3, 3, step)

    lam = lam_ref[0:1, 0:1]
    outs = []
    for hh in range(DIFF_GROUP):
        a1 = acc_s[2 * hh] * (1.0 / l_s[2 * hh])
        a2 = acc_s[2 * hh + 1] * (1.0 / l_s[2 * hh + 1])
        o = a1 - lam * a2
        ms = jnp.mean(o * o, axis=0, keepdims=True)
        outs.append(o * lax.rsqrt(ms + EPS) * subln_ref[...] * out_scale)
    o_ref[...] = jnp.concatenate(outs, axis=0).T


def _diff_attn(lam, qt, k, vt, bias, subln, batch, seq, out_scale):
    nq = seq // T
    n_tok = batch * seq
    width = DIFF_GROUP * LANES
    n_chains = 2 * DIFF_GROUP
    row_f32 = pltpu.VMEM((n_chains, 1, T), F32)
    return pl.pallas_call(
        functools.partial(_diff_attn_kernel, out_scale=out_scale),
        grid=(batch, DIFF_HEADS // DIFF_GROUP, nq),
        in_specs=[pl.BlockSpec((8, LANES), lambda b, h, i: (0, 0)),
                  pl.BlockSpec((1, width, T), lambda b, h, i: (b * nq + i, h, 0)),
                  pl.BlockSpec((seq, width), lambda b, h, i: (b, h)),
                  pl.BlockSpec((nq, width, T), lambda b, h, i: (b, h, 0)),
                  pl.BlockSpec((DIFF_GROUP, 2, T, T), lambda b, h, i: (h, 0, 0, 0)),
                  pl.BlockSpec((DIFF_V_DIM, 1), lambda b, h, i: (0, 0))],
        out_specs=pl.BlockSpec((T, width), lambda b, h, i: (b * nq + i, h)),
        out_shape=jax.ShapeDtypeStruct((n_tok, D_MODEL), F32),
        scratch_shapes=[pltpu.VMEM((2, n_chains, T, T), F32),
                        pltpu.VMEM((n_chains, T, T), BF16),
                        pltpu.VMEM((n_chains, DIFF_V_DIM, T), F32),
                        row_f32, row_f32, row_f32],
        compiler_params=pltpu.CompilerParams(dimension_semantics=("arbitrary",) * 3,
                                             vmem_limit_bytes=VMEM_LIMIT),
        name="diff_attn",
    )(lam, qt, k, vt, bias, subln)


def _final_kernel(h_ref, o_ref, g_ref, p_ref, ple_gain_ref, fin_gain_ref,
                  w_out_ref, w_gate_ref, w_ple_ref, out_ref):
    y = (o_ref[...] * _silu(g_ref[...])).astype(BF16)
    h = h_ref[...] + _dot(y, w_out_ref[...])
    h = _ple_update(h, p_ref[...], ple_gain_ref[...], w_gate_ref[...], w_ple_ref[...])
    out_ref[...] = _rms(h, fin_gain_ref[...])


def _final(h, o, g, p, ple_gain, fin_gain, w_out, w_gate, w_ple):
    n_tok = h.shape[0]
    tm = TM_PROJ
    row = lambda w: pl.BlockSpec((tm, w), lambda i: (i, 0))
    sq = _const_spec((D_MODEL, D_MODEL))
    vec = _const_spec((1, D_MODEL))
    return pl.pallas_call(
        _final_kernel,
        grid=(n_tok // tm,),
        in_specs=[row(D_MODEL), row(D_MODEL), row(D_MODEL), row(PLE_DIM), vec, vec,
                  sq, sq, _const_spec((PLE_DIM, D_MODEL))],
        out_specs=row(D_MODEL),
        out_shape=jax.ShapeDtypeStruct((n_tok, D_MODEL), F32),
        compiler_params=pltpu.CompilerParams(dimension_semantics=("arbitrary",),
                                             vmem_limit_bytes=VMEM_LIMIT),
        name="final",
    )(h, o, g, p, ple_gain, fin_gain, w_out, w_gate, w_ple)


def _interleave_heads(w_a, w_b, heads, dim):
    d_in = w_a.shape[0]
    stacked = jnp.stack([w_a.reshape(d_in, heads, dim), w_b.reshape(d_in, heads, dim)], axis=2)
    return stacked.reshape(d_in, heads * 2 * dim)


def kernel(x, p, attn_norm, w_in_a, w_out_a, kv_norm, w_kv, w_q_b, diff_lambda, diff_subln, w_out_b, rel_bias,
           ple_norm, w_ple_proj, w_ple_gate, final_norm):
    batch, seq, d = x.shape
    depth = p.shape[0]
    assert d == D_MODEL and depth == 2 and w_in_a.shape[0] == 1 and w_q_b.shape[0] == 1
    assert seq % T == 0 and (batch * seq) % TM_PROJ == 0
    n_tok = batch * seq
    x2 = x.reshape(n_tok, d)
    p2 = p.reshape(depth, n_tok, PLE_DIM)
    vec = lambda v: v.reshape(1, -1).astype(F32)
    qk_width = DIFF_HEADS * DIFF_QK_DIM

    wq, wk, wv, wg = jnp.split(w_in_a[0], 4, axis=-1)
    wk1, wk2, wvv = jnp.split(w_kv, [qk_width, 2 * qk_width], axis=-1)
    wq1, wq2, wg_b = jnp.split(w_q_b[0], [qk_width, 2 * qk_width], axis=-1)
    w_k_b = _interleave_heads(wk1, wk2, DIFF_HEADS, DIFF_QK_DIM).astype(BF16)
    w_qt_b = _interleave_heads(wq1, wq2, DIFF_HEADS, DIFF_QK_DIM).T.astype(BF16)
    w_vt_b = wvv.T.astype(BF16)

    lam_init = 0.8 - 0.6 * math.exp(-0.3 * 1)
    bias, lam = _bias_lam(rel_bias.astype(F32), diff_lambda[0].astype(F32), lam_init)

    k_a, g_a, qt_a, vt_a = _inproj_a(x2, vec(attn_norm[0]), wk.astype(BF16), wg.astype(BF16),
                                     wq.T.astype(BF16), wv.T.astype(BF16), 1.0 / math.sqrt(SB_HEAD_DIM))
    o_a = _sb_attn(qt_a, k_a, vt_a, batch, seq)

    h1, k_b, vt_b, qt_b, g_b = _mid(
        x2, o_a, g_a, p2[0], vec(ple_norm[0]), vec(kv_norm), vec(attn_norm[1]),
        w_out_a[0].astype(BF16), w_ple_gate[0].astype(BF16), w_ple_proj[0].astype(BF16),
        w_k_b, w_vt_b, w_qt_b, wg_b.astype(BF16), 1.0 / math.sqrt(DIFF_QK_DIM))

    o_b = _diff_attn(lam, qt_b, k_b, vt_b, bias, diff_subln[0].reshape(DIFF_V_DIM, 1).astype(F32),
                     batch, seq, 1.0 - lam_init)

    out = _final(h1, o_b, g_b, p2[1], vec(ple_norm[1]), vec(final_norm),
                 w_out_b[0].astype(BF16), w_ple_gate[1].astype(BF16), w_ple_proj[1].astype(BF16))
    return out.reshape(batch, seq, d)
```

```python
import functools
import math

import jax
import jax.numpy as jnp
import numpy as np
from jax import lax
from jax.experimental import pallas as pl
from jax.experimental.pallas import tpu as pltpu

D_MODEL = 1024
PLE_DIM = 256
SB_HEAD_DIM = 64
DIFF_QK_DIM = 64
DIFF_V_DIM = 128
DIFF_HEADS = 8
N_BUCKETS = 32
MAX_DISTANCE = 128
EPS = 1e-6
NEG_INF = -1e30

LANES = 128
T = 256
SB_GROUP = 4
SB_DEAD_RUN = 128.0
DIFF_GROUP = 2
TM_PROJ = 512
TM_MID = 512
V7X_VMEM_BYTES = 64 * 1024 * 1024
VMEM_LIMIT = V7X_VMEM_BYTES * 7 // 8

F32 = jnp.float32
BF16 = jnp.bfloat16
NT_DIMS = (((1,), (1,)), ((), ()))


def _rms(x, gain):
    ms = jnp.mean(x * x, axis=-1, keepdims=True)
    return x * lax.rsqrt(ms + EPS) * gain


def _silu(g):
    return g * jax.nn.sigmoid(g)


def _dot(a, b):
    return jnp.dot(a, b, preferred_element_type=F32)


def _dot_nt(a, b):
    return lax.dot_general(a, b, NT_DIMS, preferred_element_type=F32)


def _store_token_major_t(ref, val_t, scale=None):
    for c in range(val_t.shape[1] // T):
        blk = val_t[:, c * T:(c + 1) * T]
        if scale is not None:
            blk = blk * scale
        ref[c] = blk.astype(BF16)


def _const_spec(shape):
    return pl.BlockSpec(shape, lambda *_: (0,) * len(shape), pipeline_mode=pl.Buffered(1))


def _pipeline(n, n_stages, peel, step, exhausted=None):
    last = n_stages - 1
    stop0 = None
    for t in range(peel):
        if exhausted is not None and t == peel - 1 and t >= last:
            step(t, last - 1, last, t % 2)
            stop0 = jnp.where(n >= peel - last + 1, exhausted(), 0)
            pl.when(stop0 == 0)(functools.partial(step, t, 0, last - 2, t % 2))
        else:
            step(t, 0, min(t, last), t % 2)

    def pair(tt):
        t = peel + 2 * tt
        step(t, 0, last, peel % 2)
        step(t + 1, 0, last, 1 - peel % 2)

    steady = jnp.maximum(n - peel, 0)
    pairs = steady // 2
    if exhausted is None:
        lax.fori_loop(0, pairs, lambda tt, c: (pair(tt), c)[1], 0)
        live = None
    else:
        def body(carry):
            tt, _ = carry
            pair(tt)
            return tt + 1, exhausted()

        if stop0 is None:
            stop0 = jnp.int32(0)
        trips, stop = lax.while_loop(lambda c: (c[0] < pairs) & (c[1] == 0), body, (jnp.int32(0), stop0))
        live = stop == 0

        @pl.when(stop == 1)
        def _():
            step(peel + 2 * trips, last, last, None)

    def when_live(cond):
        return pl.when(cond if live is None else (live & cond))

    @when_live(steady % 2 == 1)
    def _():
        step(n - 1, 0, last, peel % 2)

    for d in range(last):
        @when_live(n + d >= peel)
        def _():
            step(n + d, d + 1, last, None)


def _bucket_tiles():
    key = np.arange(T, dtype=np.int32)[:, None]
    qry = np.arange(T, dtype=np.int32)[None, :]
    max_exact = N_BUCKETS // 2
    tiles = []
    for d in range(2):
        dist = qry - key + d * T
        n = np.maximum(dist, 0)
        nf = np.maximum(n, 1).astype(np.float32)
        large = max_exact + (np.log(nf / np.float32(max_exact)) / np.float32(math.log(MAX_DISTANCE / max_exact))
                             * np.float32(N_BUCKETS - max_exact)).astype(np.int32)
        large = np.minimum(large, N_BUCKETS - 1)
        tiles.append(np.where(dist >= 0, np.where(n < max_exact, n, large), -1).astype(np.int32))
    return np.stack(tiles)


def _bias_lam_kernel(bucket_ref, table_ref, lp_ref, bias_ref, lam_ref, *, lam_init):
    table = table_ref[0]
    far = table[:, N_BUCKETS - 1:N_BUCKETS]
    for d in range(2):
        bucket = bucket_ref[d]
        val = jnp.zeros((T, T), F32)
        for b in range(N_BUCKETS):
            val = jnp.where(bucket == b, table[:, b:b + 1], val)
        bias_ref[0, d] = jnp.where(bucket >= 0, val - far, NEG_INF)

    lp = lp_ref[...]
    s1 = jnp.sum(lp[0:1, :] * lp[1:2, :], axis=-1, keepdims=True)
    s2 = jnp.sum(lp[2:3, :] * lp[3:4, :], axis=-1, keepdims=True)
    lam = jnp.exp(s1) - jnp.exp(s2) + lam_init
    lam_ref[...] = jnp.broadcast_to(lam, lam_ref.shape)


def _bias_lam(rel_bias, lp, lam_init):
    table = rel_bias.T.reshape(DIFF_HEADS, 1, N_BUCKETS)
    return pl.pallas_call(
        functools.partial(_bias_lam_kernel, lam_init=lam_init),
        grid=(DIFF_HEADS,),
        in_specs=[pl.BlockSpec((2, T, T), lambda h: (0, 0, 0)),
                  pl.BlockSpec((1, 1, N_BUCKETS), lambda h: (h, 0, 0)),
                  pl.BlockSpec((4, DIFF_QK_DIM), lambda h: (0, 0))],
        out_specs=[pl.BlockSpec((1, 2, T, T), lambda h: (h, 0, 0, 0)),
                   pl.BlockSpec((8, LANES), lambda h: (0, 0))],
        out_shape=[jax.ShapeDtypeStruct((DIFF_HEADS, 2, T, T), F32),
                   jax.ShapeDtypeStruct((8, LANES), F32)],
        name="bias_lam",
    )(jnp.asarray(_bucket_tiles()), table, lp)


def _inproj_a_kernel(x_ref, gain_ref, wkg_ref, wqvt_ref, k_ref, g_ref, qt_ref, vt_ref, *, scale):
    hn = _rms(x_ref[...], gain_ref[...]).astype(BF16)
    k_ref[...] = _dot(hn, wkg_ref[:, :D_MODEL]).astype(BF16)
    g_ref[...] = _dot(hn, wkg_ref[:, D_MODEL:])
    _store_token_major_t(qt_ref, _dot_nt(wqvt_ref[:D_MODEL, :], hn), scale)
    _store_token_major_t(vt_ref, _dot_nt(wqvt_ref[D_MODEL:, :], hn))


def _inproj_a(x2, gain, wkg, wqvt, scale):
    n_tok = x2.shape[0]
    tm = TM_PROJ
    return pl.pallas_call(
        functools.partial(_inproj_a_kernel, scale=scale),
        grid=(n_tok // tm,),
        in_specs=[pl.BlockSpec((tm, D_MODEL), lambda i: (i, 0)),
                  _const_spec((1, D_MODEL)),
                  _const_spec((D_MODEL, 2 * D_MODEL)),
                  _const_spec((2 * D_MODEL, D_MODEL))],
        out_specs=[pl.BlockSpec((tm, D_MODEL), lambda i: (i, 0)),
                   pl.BlockSpec((tm, D_MODEL), lambda i: (i, 0)),
                   pl.BlockSpec((tm // T, D_MODEL, T), lambda i: (i, 0, 0)),
                   pl.BlockSpec((tm // T, D_MODEL, T), lambda i: (i, 0, 0))],
        out_shape=[jax.ShapeDtypeStruct((n_tok, D_MODEL), BF16),
                   jax.ShapeDtypeStruct((n_tok, D_MODEL), F32),
                   jax.ShapeDtypeStruct((n_tok // T, D_MODEL, T), BF16),
                   jax.ShapeDtypeStruct((n_tok // T, D_MODEL, T), BF16)],
        compiler_params=pltpu.CompilerParams(dimension_semantics=("arbitrary",),
                                             vmem_limit_bytes=VMEM_LIMIT),
        name="inproj_a",
    )(x2, gain, wkg, wqvt)


def _sb_attn_kernel(qt_ref, k_ref, vt_ref, o_ref,
                    z_s, sp_s, ls_s, w_s, acc_s, run_s, scale_s, sp0_s):
    i = pl.program_id(2)
    n = i + 1
    hd = SB_HEAD_DIM
    heads = range(SB_GROUP)
    row = lax.broadcasted_iota(jnp.int32, (LANES, T), 0)
    q_heads = []
    for hh in heads:
        g, half = divmod(hh, 2)
        qg = qt_ref[0, g * LANES:(g + 1) * LANES, :]
        keep = (row < hd) if half == 0 else (row >= hd)
        q_heads.append(jnp.where(keep, qg, jnp.zeros_like(qg)))
    key = lax.broadcasted_iota(jnp.int32, (T, T), 0)
    qry = lax.broadcasted_iota(jnp.int32, (T, T), 1)
    upper = jnp.where(qry > key, 1.0, 0.0).astype(BF16)
    past = key < qry

    def stage_p(t):
        j = i - (t - 3)
        for hh in heads:
            vb = vt_ref[j, hh * hd:(hh + 1) * hd, :]
            acc_s[hh] = acc_s[hh] + _dot(vb, w_s[hh]) * scale_s[hh]

    def stage_c():
        return [_dot(upper, sp_s[hh]) for hh in heads]

    def stage_e(c, slot, diag):
        for hh in heads:
            w = jnp.exp(ls_s[slot, hh] - c[hh])
            if diag:
                w = jnp.where(past, w, 0.0)
            w_s[hh] = w.astype(BF16)
            run = run_s[hh]
            scale_s[hh] = jnp.exp(-run)
            run_s[hh] = run + c[hh][0:1, :] + sp0_s[hh]

    def stage_s(slot, diag):
        sp0 = []
        for hh in heads:
            z = z_s[slot, hh]
            soft = jnp.log(1.0 + jnp.exp(-jnp.abs(z)))
            sp = jnp.maximum(z, 0.0) + soft
            ls_s[slot, hh] = z - sp
            if diag:
                sp = jnp.where(past, sp, 0.0)
            sp_s[hh] = sp.astype(BF16)
            sp0.append(sp[0:1, :])
        return sp0

    def stage_z(t, slot):
        j = jnp.maximum(i - t, 0)
        start = pl.multiple_of(j * T, T)
        for hh in heads:
            g = hh // 2
            z_s[slot, hh] = _dot(k_ref[pl.ds(start, T), g * LANES:(g + 1) * LANES], q_heads[hh])

    def step(t, first, last, parity):
        now = (t & 1) if parity is None else parity
        prev = 1 - now
        peeled = isinstance(t, int)
        if first <= 3 <= last:
            stage_p(t)
        c = stage_c() if first <= 2 <= last else None
        if first <= 0 <= last:
            stage_z(t, now)
        sp0 = stage_s(prev, diag=(peeled and t == 1)) if first <= 1 <= last else None
        if c is not None:
            stage_e(c, now, diag=(peeled and t == 2))
        if sp0 is not None:
            for hh in heads:
                sp0_s[hh] = sp0[hh]

    for hh in heads:
        acc_s[hh] = jnp.zeros((hd, T), F32)
        run_s[hh] = jnp.zeros((1, T), F32)

    def exhausted():
        return (jnp.min(run_s[...]) >= SB_DEAD_RUN).astype(jnp.int32)

    _pipeline(n, 4, 4, step, exhausted)

    o_t = jnp.concatenate([acc_s[hh] for hh in heads], axis=0)
    o_ref[...] = o_t.T


def _sb_attn(qt, k, vt, batch, seq):
    nq = seq // T
    n_tok = batch * seq
    width = SB_GROUP * SB_HEAD_DIM
    groups = D_MODEL // width
    tile_bf16 = pltpu.VMEM((SB_GROUP, T, T), BF16)
    pingpong_f32 = pltpu.VMEM((2, SB_GROUP, T, T), F32)
    row_f32 = pltpu.VMEM((SB_GROUP, 1, T), F32)
    return pl.pallas_call(
        _sb_attn_kernel,
        grid=(batch, groups, nq),
        in_specs=[pl.BlockSpec((1, width, T), lambda b, h, i: (b * nq + i, h, 0)),
                  pl.BlockSpec((seq, width), lambda b, h, i: (b, h)),
                  pl.BlockSpec((nq, width, T), lambda b, h, i: (b, h, 0))],
        out_specs=pl.BlockSpec((T, width), lambda b, h, i: (b * nq + i, h)),
        out_shape=jax.ShapeDtypeStruct((n_tok, D_MODEL), F32),
        scratch_shapes=[pingpong_f32,
                        tile_bf16,
                        pingpong_f32,
                        tile_bf16,
                        pltpu.VMEM((SB_GROUP, SB_HEAD_DIM, T), F32),
                        row_f32, row_f32, row_f32],
        compiler_params=pltpu.CompilerParams(dimension_semantics=("arbitrary",) * 3,
                                             vmem_limit_bytes=VMEM_LIMIT),
        name="sb_attn",
    )(qt, k, vt)


def _ple_update(h, p, ple_gain, w_gate, w_ple):
    gate = jax.nn.sigmoid(_dot(_rms(h, ple_gain).astype(BF16), w_gate))
    return h + gate * _dot(p.astype(BF16), w_ple)


def _mid_kernel(x_ref, o_ref, g_ref, p_ref, ple_gain_ref, kv_gain_ref, q_gain_ref,
                w_out_ref, w_gate_ref, w_ple_ref, w_k_ref, w_vt_ref, w_qt_ref, w_g_ref,
                h_ref, k_ref, vt_ref, qt_ref, g2_ref, *, scale):
    y = (o_ref[...] * _silu(g_ref[...])).astype(BF16)
    h = x_ref[...] + _dot(y, w_out_ref[...])
    h = _ple_update(h, p_ref[0], ple_gain_ref[...], w_gate_ref[...], w_ple_ref[...])
    h_ref[...] = h
    hk = _rms(h, kv_gain_ref[...]).astype(BF16)
    k_ref[...] = _dot(hk, w_k_ref[...]).astype(BF16)
    _store_token_major_t(vt_ref, _dot_nt(w_vt_ref[...], hk))
    hq = _rms(h, q_gain_ref[...]).astype(BF16)
    _store_token_major_t(qt_ref, _dot_nt(w_qt_ref[...], hq), scale)
    g2_ref[...] = _dot(hq, w_g_ref[...])


def _mid(x2, o, g, p, ple_gain, kv_gain, q_gain, w_out, w_gate, w_ple, w_k, w_vt, w_qt, w_g, scale):
    n_tok = x2.shape[0]
    tm = TM_MID
    row = lambda w: pl.BlockSpec((tm, w), lambda i: (i, 0))
    sq = _const_spec((D_MODEL, D_MODEL))
    vec = _const_spec((1, D_MODEL))
    tmaj = pl.BlockSpec((tm // T, D_MODEL, T), lambda i: (i, 0, 0))
    return pl.pallas_call(
        functools.partial(_mid_kernel, scale=scale),
        grid=(n_tok // tm,),
        in_specs=[row(D_MODEL), row(D_MODEL), row(D_MODEL),
                  pl.BlockSpec((1, tm, PLE_DIM), lambda i: (0, i, 0)),
                  vec, vec, vec,
                  sq, sq, _const_spec((PLE_DIM, D_MODEL)), sq, sq, sq, sq],
        out_specs=[row(D_MODEL), row(D_MODEL), tmaj, tmaj, row(D_MODEL)],
        out_shape=[jax.ShapeDtypeStruct((n_tok, D_MODEL), F32),
                   jax.ShapeDtypeStruct((n_tok, D_MODEL), BF16),
                   jax.ShapeDtypeStruct((n_tok // T, D_MODEL, T), BF16),
                   jax.ShapeDtypeStruct((n_tok // T, D_MODEL, T), BF16),
                   jax.ShapeDtypeStruct((n_tok, D_MODEL), F32)],
        compiler_params=pltpu.CompilerParams(dimension_semantics=("arbitrary",),
                                             vmem_limit_bytes=VMEM_LIMIT),
        name="mid",
    )(x2, o, g, p, ple_gain, kv_gain, q_gain, w_out, w_gate, w_ple, w_k, w_vt, w_qt, w_g)


def _diff_attn_kernel(lam_ref, qt_ref, k_ref, vt_ref, bias_ref, subln_ref, o_ref,
                      s_s, p_s, acc_s, m_s, l_s, alpha_s, *, out_scale):
    i = pl.program_id(2)
    n = i + 1
    dk = DIFF_QK_DIM
    chains = [(hh, mm) for hh in range(DIFF_GROUP) for mm in range(2)]
    row = lax.broadcasted_iota(jnp.int32, (LANES, T), 0)
    q_maps = []
    for hh, mm in chains:
        qh = qt_ref[0, hh * LANES:(hh + 1) * LANES, :]
        keep = (row < dk) if mm == 0 else (row >= dk)
        q_maps.append(jnp.where(keep, qh, jnp.zeros_like(qh)))

    def stage_p(t):
        j = i - (t - 2)
        for c, (hh, mm) in enumerate(chains):
            vb = vt_ref[j, hh * LANES:(hh + 1) * LANES, :]
            acc_s[c] = acc_s[c] * alpha_s[c] + _dot(vb, p_s[c])

    def stage_v(slot):
        for c in range(len(chains)):
            m = m_s[c]
            s = s_s[slot, c]
            m_new = jnp.maximum(m, jnp.max(s, axis=0, keepdims=True))
            p = jnp.exp(s - m_new)
            alpha = jnp.exp(m - m_new)
            p_s[c] = p.astype(BF16)
            alpha_s[c] = alpha
            m_s[c] = m_new
            l_s[c] = l_s[c] * alpha + jnp.sum(p, axis=0, keepdims=True)

    def stage_s(t, slot, near):
        j = jnp.maximum(i - t, 0)
        start = pl.multiple_of(j * T, T)
        for c, (hh, mm) in enumerate(chains):
            s = _dot(k_ref[pl.ds(start, T), hh * LANES:(hh + 1) * LANES], q_maps[c])
            if near is not None:
                s = s + bias_ref[hh, near]
            s_s[slot, c] = s

    def step(t, first, last, parity):
        now = (t & 1) if parity is None else parity
        peeled = isinstance(t, int)
        if first <= 0 <= last:
            stage_s(t, now, t if peeled and t < 2 else None)
        if first <= 2 <= last:
            stage_p(t)
        if first <= 1 <= last:
            if peeled and t == 2:
                pl.when(n > 1)(lambda: stage_v(1 - now))
            else:
                stage_v(1 - now)

    for c in range(len(chains)):
        acc_s[c] = jnp.zeros((DIFF_V_DIM, T), F32)
        m_s[c] = jnp.full((1, T), NEG_INF, F32)
        l_s[c] = jnp.zeros((1, T), F32)

    _pipeline(n, 3, 3, step)

    lam = lam_ref[0:1, 0:1]
    outs = []
    for hh in range(DIFF_GROUP):
        a1 = acc_s[2 * hh] * (1.0 / l_s[2 * hh])
        a2 = acc_s[2 * hh + 1] * (1.0 / l_s[2 * hh + 1])
        o = a1 - lam * a2
        ms = jnp.mean(o * o, axis=0, keepdims=True)
        outs.append(o * lax.rsqrt(ms + EPS) * subln_ref[...] * out_scale)
    o_ref[...] = jnp.concatenate(outs, axis=0).T


def _diff_attn(lam, qt, k, vt, bias, subln, batch, seq, out_scale):
    nq = seq // T
    n_tok = batch * seq
    width = DIFF_GROUP * LANES
    n_chains = 2 * DIFF_GROUP
    row_f32 = pltpu.VMEM((n_chains, 1, T), F32)
    return pl.pallas_call(
        functools.partial(_diff_attn_kernel, out_scale=out_scale),
        grid=(batch, DIFF_HEADS // DIFF_GROUP, nq),
        in_specs=[pl.BlockSpec((8, LANES), lambda b, h, i: (0, 0)),
                  pl.BlockSpec((1, width, T), lambda b, h, i: (b * nq + i, h, 0)),
                  pl.BlockSpec((seq, width), lambda b, h, i: (b, h)),
                  pl.BlockSpec((nq, width, T), lambda b, h, i: (b, h, 0)),
                  pl.BlockSpec((DIFF_GROUP, 2, T, T), lambda b, h, i: (h, 0, 0, 0)),
                  pl.BlockSpec((DIFF_V_DIM, 1), lambda b, h, i: (0, 0))],
        out_specs=pl.BlockSpec((T, width), lambda b, h, i: (b * nq + i, h)),
        out_shape=jax.ShapeDtypeStruct((n_tok, D_MODEL), F32),
        scratch_shapes=[pltpu.VMEM((2, n_chains, T, T), F32),
                        pltpu.VMEM((n_chains, T, T), BF16),
                        pltpu.VMEM((n_chains, DIFF_V_DIM, T), F32),
                        row_f32, row_f32, row_f32],
        compiler_params=pltpu.CompilerParams(dimension_semantics=("arbitrary",) * 3,
                                             vmem_limit_bytes=VMEM_LIMIT),
        name="diff_attn",
    )(lam, qt, k, vt, bias, subln)


def _final_kernel(h_ref, o_ref, g_ref, p_ref, ple_gain_ref, fin_gain_ref,
                  w_out_ref, w_gate_ref, w_ple_ref, out_ref):
    y = (o_ref[...] * _silu(g_ref[...])).astype(BF16)
    h = h_ref[...] + _dot(y, w_out_ref[...])
    h = _ple_update(h, p_ref[0], ple_gain_ref[...], w_gate_ref[...], w_ple_ref[...])
    out_ref[...] = _rms(h, fin_gain_ref[...])


def _final(h, o, g, p, ple_gain, fin_gain, w_out, w_gate, w_ple):
    n_tok = h.shape[0]
    tm = TM_PROJ
    row = lambda w: pl.BlockSpec((tm, w), lambda i: (i, 0))
    sq = _const_spec((D_MODEL, D_MODEL))
    vec = _const_spec((1, D_MODEL))
    return pl.pallas_call(
        _final_kernel,
        grid=(n_tok // tm,),
        in_specs=[row(D_MODEL), row(D_MODEL), row(D_MODEL),
                  pl.BlockSpec((1, tm, PLE_DIM), lambda i: (1, i, 0)),
                  vec, vec,
                  sq, sq, _const_spec((PLE_DIM, D_MODEL))],
        out_specs=row(D_MODEL),
        out_shape=jax.ShapeDtypeStruct((n_tok, D_MODEL), F32),
        compiler_params=pltpu.CompilerParams(dimension_semantics=("arbitrary",),
                                             vmem_limit_bytes=VMEM_LIMIT),
        name="final",
    )(h, o, g, p, ple_gain, fin_gain, w_out, w_gate, w_ple)


def _interleave_heads(w_a, w_b, heads, dim):
    d_in = w_a.shape[0]
    stacked = jnp.stack([w_a.reshape(d_in, heads, dim), w_b.reshape(d_in, heads, dim)], axis=2)
    return stacked.reshape(d_in, heads * 2 * dim)


def kernel(x, p, attn_norm, w_in_a, w_out_a, kv_norm, w_kv, w_q_b, diff_lambda, diff_subln, w_out_b, rel_bias,
           ple_norm, w_ple_proj, w_ple_gate, final_norm):
    batch, seq, d = x.shape
    depth = p.shape[0]
    assert d == D_MODEL and depth == 2 and w_in_a.shape[0] == 1 and w_q_b.shape[0] == 1
    assert seq % T == 0 and (batch * seq) % TM_PROJ == 0
    n_tok = batch * seq
    x2 = x.reshape(n_tok, d)
    p2 = p.reshape(depth, n_tok, PLE_DIM)
    vec = lambda v: v.reshape(1, -1).astype(F32)
    qk_width = DIFF_HEADS * DIFF_QK_DIM

    wq, wk, wv, wg = jnp.split(w_in_a[0], 4, axis=-1)
    wkg_a = jnp.concatenate([wk, wg], axis=1).astype(BF16)
    wqvt_a = jnp.concatenate([wq, wv], axis=1).T.astype(BF16)
    wk1, wk2, wvv = jnp.split(w_kv, [qk_width, 2 * qk_width], axis=-1)
    wq1, wq2, wg_b = jnp.split(w_q_b[0], [qk_width, 2 * qk_width], axis=-1)
    w_k_b = _interleave_heads(wk1, wk2, DIFF_HEADS, DIFF_QK_DIM).astype(BF16)
    w_qt_b = _interleave_heads(wq1, wq2, DIFF_HEADS, DIFF_QK_DIM).T.astype(BF16)
    w_vt_b = wvv.T.astype(BF16)

    lam_init = 0.8 - 0.6 * math.exp(-0.3 * 1)
    bias, lam = _bias_lam(rel_bias.astype(F32), diff_lambda[0].astype(F32), lam_init)

    k_a, g_a, qt_a, vt_a = _inproj_a(x2, vec(attn_norm[0]), wkg_a, wqvt_a, 1.0 / math.sqrt(SB_HEAD_DIM))
    o_a = _sb_attn(qt_a, k_a, vt_a, batch, seq)

    h1, k_b, vt_b, qt_b, g_b = _mid(
        x2, o_a, g_a, p2, vec(ple_norm[0]), vec(kv_norm), vec(attn_norm[1]),
        w_out_a[0].astype(BF16), w_ple_gate[0].astype(BF16), w_ple_proj[0].astype(BF16),
        w_k_b, w_vt_b, w_qt_b, wg_b.astype(BF16), 1.0 / math.sqrt(DIFF_QK_DIM))

    o_b = _diff_attn(lam, qt_b, k_b, vt_b, bias, diff_subln[0].reshape(DIFF_V_DIM, 1).astype(F32),
                     batch, seq, 1.0 - lam_init)

    out = _final(h1, o_b, g_b, p2, vec(ple_norm[1]), vec(final_norm),
                 w_out_b[0].astype(BF16), w_ple_gate[1].astype(BF16), w_ple_proj[1].astype(BF16))
    return out.reshape(batch, seq, d)
```
